```python
import math
import jax, jax.numpy as jnp
from jax import lax
import numpy as np

D_MODEL = 1024
BATCH = 16
SEQ = 2048
DEPTH = 2
DEC_BATCH = 8
DEC_SEQ = 4096
PAST_LEN = 128

A_HEADS = 4
A_QK_DIM = 64
A_V_DIM = 2 * A_QK_DIM
B_HEADS = 4
B_K_DIM = 64
B_V_DIM = 128
B_GATE_RANK = 16
B_GATE_NORM = 16.0
B_CHUNK = 64
C_HEADS = 4
C_QK_DIM = 128
C_V_DIM = 256
C_CHUNK = 64
C_FGATE_BIAS_LO = 3.0
C_FGATE_BIAS_HI = 6.0
D_FF = 256 * (-(-8 * D_MODEL // (3 * 256)))
Q_BLOCK = 128
EPS = 1e-6
N_EVEN = (DEPTH + 1) // 2
N_ODD = DEPTH // 2

EVEN_IN_SIZES = (A_HEADS * 2 * A_QK_DIM, A_HEADS * 2 * A_QK_DIM, A_HEADS * A_V_DIM,
                 B_HEADS * B_K_DIM, B_HEADS * B_K_DIM, B_HEADS * B_V_DIM, B_HEADS * B_V_DIM, 2 * B_GATE_RANK)
EVEN_IN = sum(EVEN_IN_SIZES)
EVEN_MIX = A_HEADS * A_V_DIM + B_HEADS * B_V_DIM
ODD_IN_SIZES = (C_HEADS * C_QK_DIM, C_HEADS * C_QK_DIM, C_HEADS * C_V_DIM, C_HEADS * C_V_DIM, 4 * C_HEADS)
ODD_IN = sum(ODD_IN_SIZES)
ODD_MIX = C_HEADS * C_V_DIM

kernel_name = 'hybrid_diffattn_gla_mlstm_encoder'


def split_cols(z, sizes):
    offs = np.cumsum(sizes)[:-1].tolist()
    return jnp.split(z, offs, axis=-1)


def rmsnorm(x, g):
    xf = x.astype(jnp.float32)
    y = xf * lax.rsqrt(jnp.mean(xf * xf, axis=-1, keepdims=True) + EPS)
    return (y * g.astype(jnp.float32)).astype(x.dtype)


def alibi_slopes(n):
    return jnp.exp2(-8.0 * jnp.arange(1, n + 1, dtype=jnp.float32) / n)


def diff_attention(q, k, v, lam, sub_g, lam_init):
    B, T = q.shape[0], q.shape[1]
    nb = T // Q_BLOCK
    qf = q.astype(jnp.float32) * (A_QK_DIM ** -0.5)
    kf = k.astype(jnp.float32)
    vf = v.astype(jnp.float32)
    slopes = alibi_slopes(A_HEADS)
    kpos = jnp.arange(T, dtype=jnp.float32)
    qb = jnp.moveaxis(qf.reshape((B, nb, Q_BLOCK) + qf.shape[2:]), 1, 0)
    starts = jnp.arange(nb, dtype=jnp.float32) * Q_BLOCK

    def block(args):
        qblk, start = args
        qpos = start + jnp.arange(Q_BLOCK, dtype=jnp.float32)
        bias = -slopes[:, None, None] * jnp.abs(qpos[:, None] - kpos[None, :])
        s = jnp.einsum('bqhcd,bkhcd->bhcqk', qblk, kf) + bias[None, :, None]
        p = jax.nn.softmax(s, axis=-1)
        w = p[:, :, 0] - lam * p[:, :, 1]
        return jnp.einsum('bhqk,bkhe->bqhe', w, vf)

    out = lax.map(block, (qb, starts))
    out = jnp.moveaxis(out, 0, 1).reshape(B, T, A_HEADS, A_V_DIM)
    out = rmsnorm(out, sub_g) * (1.0 - lam_init)
    return out.reshape(B, T, A_HEADS * A_V_DIM)


def gla_direction(q, k, v, lg):
    B, H, T, dk = q.shape
    dv = v.shape[-1]
    L = B_CHUNK
    N = T // L
    r = lambda t: t.reshape(B, H, N, L, t.shape[-1])
    q, k, v, lg = r(q), r(k), r(v), r(lg)
    b = jnp.cumsum(lg, axis=3)
    b_last = b[:, :, :, -1:, :]
    qe = q * jnp.exp(b)
    ke = k * jnp.exp(-b)
    mask = jnp.tril(jnp.ones((L, L), dtype=bool))
    att = jnp.where(mask, jnp.einsum('bhntd,bhnsd->bhnts', qe, ke), 0.0)
    intra = jnp.einsum('bhnts,bhnse->bhnte', att, v)
    u = jnp.einsum('bhnsd,bhnse->bhnde', k * jnp.exp(b_last - b), v)
    decay = jnp.exp(b_last[:, :, :, 0, :])

    def step(S, xs):
        dec, uc = xs
        return dec[..., None] * S + uc, S

    _, S_prev = lax.scan(step, jnp.zeros((B, H, dk, dv), jnp.float32),
                         (jnp.moveaxis(decay, 2, 0), jnp.moveaxis(u, 2, 0)))
    S_prev = jnp.moveaxis(S_prev, 0, 2)
    inter = jnp.einsum('bhntd,bhnde->bhnte', qe, S_prev)
    return (inter + intra).reshape(B, H, T, dv)


def mlstm_direction(q, k, v, ig, lf):
    B, H, T, dk = q.shape
    dv = v.shape[-1]
    L = C_CHUNK
    N = T // L
    ch = lambda t: jnp.moveaxis(t.reshape(t.shape[:2] + (N, L) + t.shape[3:]), 2, 0)
    mask = jnp.tril(jnp.ones((L, L), dtype=bool))

    def step(carry, xs):
        C, n, m = carry
        qc, kc, vc, ic, fc = xs
        F = jnp.cumsum(fc, axis=-1)
        D = jnp.where(mask, F[..., :, None] - F[..., None, :] + ic[..., None, :], -jnp.inf)
        a = F + m[..., None]
        mt = jnp.maximum(a, jnp.max(D, axis=-1))
        W = jnp.exp(D - mt[..., None]) * jnp.einsum('bhtd,bhsd->bhts', qc, kc)
        inter = jnp.exp(a - mt)
        num = inter[..., None] * jnp.einsum('bhtd,bhde->bhte', qc, C) + jnp.einsum('bhts,bhse->bhte', W, vc)
        den = inter * jnp.einsum('bhtd,bhd->bht', qc, n) + jnp.sum(W, axis=-1)
        hc = num / jnp.maximum(jnp.abs(den), jnp.exp(-mt))[..., None]
        FL = F[..., -1]
        wl = FL[..., None] - F + ic
        m_new = jnp.maximum(FL + m, jnp.max(wl, axis=-1))
        dec = jnp.exp(FL + m - m_new)
        ws = jnp.exp(wl - m_new[..., None])
        C_new = dec[..., None, None] * C + jnp.einsum('bhs,bhsd,bhse->bhde', ws, kc, vc)
        n_new = dec[..., None] * n + jnp.einsum('bhs,bhsd->bhd', ws, kc)
        return (C_new, n_new, m_new), hc

    init = (jnp.zeros((B, H, dk, dv), jnp.float32), jnp.zeros((B, H, dk), jnp.float32),
            jnp.zeros((B, H), jnp.float32))
    _, hs = lax.scan(step, init, (ch(q), ch(k), ch(v), ch(ig), ch(lf)))
    return jnp.moveaxis(hs, 0, 2).reshape(B, H, T, dv)


def flip_t(t):
    return jnp.flip(t, axis=2)


def even_mixer(h, w_in, lq1, lk1, lq2, lk2, sub_g, gk_w_f, gk_b_f, gk_w_b, gk_b_b, gla_norm_g, w_out, lam_init):
    B, T, _ = h.shape
    z = h @ w_in
    aq, ak, av, bq, bk, bv, bg, blr = split_cols(z, EVEN_IN_SIZES)
    lam = (jnp.exp(jnp.sum(lq1.astype(jnp.float32) * lk1.astype(jnp.float32)))
           - jnp.exp(jnp.sum(lq2.astype(jnp.float32) * lk2.astype(jnp.float32))) + lam_init)
    a_out = diff_attention(aq.reshape(B, T, A_HEADS, 2, A_QK_DIM), ak.reshape(B, T, A_HEADS, 2, A_QK_DIM),
                           av.reshape(B, T, A_HEADS, A_V_DIM), lam, sub_g, lam_init)
    heads = lambda t, d: t.astype(jnp.float32).reshape(B, T, B_HEADS, d).transpose(0, 2, 1, 3)
    q = heads(bq, B_K_DIM) * (B_K_DIM ** -0.5)
    k = heads(bk, B_K_DIM)
    v = heads(bv, B_V_DIM)
    lr_f, lr_b = jnp.split(blr.astype(jnp.float32), 2, axis=-1)
    lg_f = heads(jax.nn.log_sigmoid(lr_f @ gk_w_f.astype(jnp.float32) + gk_b_f.astype(jnp.float32)) / B_GATE_NORM, B_K_DIM)
    lg_b = heads(jax.nn.log_sigmoid(lr_b @ gk_w_b.astype(jnp.float32) + gk_b_b.astype(jnp.float32)) / B_GATE_NORM, B_K_DIM)
    o = gla_direction(q, k, v, lg_f) + flip_t(gla_direction(flip_t(q), flip_t(k), flip_t(v), flip_t(lg_b)))
    o = rmsnorm(o.transpose(0, 2, 1, 3), gla_norm_g.reshape(B_HEADS, B_V_DIM)).reshape(B, T, B_HEADS * B_V_DIM)
    b_out = o * jax.nn.silu(bg.astype(jnp.float32))
    mix = jnp.concatenate([a_out, b_out], axis=-1).astype(h.dtype)
    return mix @ w_out


def odd_mixer(h, w_in, gate_b, norm_g, w_out):
    B, T, _ = h.shape
    z = h @ w_in
    cq, ck, cv, co, cg = split_cols(z, ODD_IN_SIZES)
    heads = lambda t, d: t.astype(jnp.float32).reshape(B, T, C_HEADS, d).transpose(0, 2, 1, 3)
    q = heads(cq, C_QK_DIM)
    k = heads(ck, C_QK_DIM) * (C_QK_DIM ** -0.5)
    v = heads(cv, C_V_DIM)
    g = (cg.astype(jnp.float32) + gate_b.astype(jnp.float32)).reshape(B, T, 4, C_HEADS).transpose(2, 0, 3, 1)
    ig_f, lf_f = g[0], jax.nn.log_sigmoid(g[1])
    ig_b, lf_b = g[2], jax.nn.log_sigmoid(g[3])
    hf = mlstm_direction(q, k, v, ig_f, lf_f)
    hb = flip_t(mlstm_direction(flip_t(q), flip_t(k), flip_t(v), flip_t(ig_b), flip_t(lf_b)))
    ht = rmsnorm((hf + hb).transpose(0, 2, 1, 3), norm_g.reshape(C_HEADS, C_V_DIM)).reshape(B, T, ODD_MIX)
    out = (jax.nn.sigmoid(co.astype(jnp.float32)) * ht).astype(h.dtype)
    return out @ w_out


def swiglu(h, w_gu, w_down):
    g, u = jnp.split(h @ w_gu, 2, axis=-1)
    return (jax.nn.silu(g) * u) @ w_down


def modulate(h, shift, scale):
    return h * (1.0 + scale[:, None, :]) + shift[:, None, :]


def trunk(x, c, w_mod, b_mod, norm1_g, norm2_g, even_w_in, even_lam_q1, even_lam_k1, even_lam_q2, even_lam_k2,
          even_attn_sub_g, even_gk_w_f, even_gk_b_f, even_gk_w_b, even_gk_b_b, even_gla_norm_g, even_w_out,
          odd_w_in, odd_gate_b, odd_norm_g, odd_w_out, ffn_w_gu, ffn_w_down, final_g):
    for i in range(DEPTH):
        mod = jax.nn.silu(c) @ w_mod[i] + b_mod[i]
        sh1, sc1, g1, sh2, sc2, g2 = jnp.split(mod, 6, axis=-1)
        h = modulate(rmsnorm(x, norm1_g[i]), sh1, sc1)
        if i % 2 == 0:
            j = i // 2
            lam_init = 0.8 - 0.6 * math.exp(-0.3 * i)
            mix = even_mixer(h, even_w_in[j], even_lam_q1[j], even_lam_k1[j], even_lam_q2[j], even_lam_k2[j],
                             even_attn_sub_g[j], even_gk_w_f[j], even_gk_b_f[j], even_gk_w_b[j], even_gk_b_b[j],
                             even_gla_norm_g[j], even_w_out[j], lam_init)
        else:
            j = i // 2
            mix = odd_mixer(h, odd_w_in[j], odd_gate_b[j], odd_norm_g[j], odd_w_out[j])
        x = x + g1[:, None, :] * mix
        h = modulate(rmsnorm(x, norm2_g[i]), sh2, sc2)
        x = x + g2[:, None, :] * swiglu(h, ffn_w_gu[i], ffn_w_down[i])
    return rmsnorm(x, final_g)


def setup_inputs(seed: int = 0) -> dict:
    key = jax.random.key(seed)
    ks = iter(jax.random.split(key, 40))
    nrm = lambda shape, scale: jax.random.normal(next(ks), shape, jnp.float32) * scale
    gain = lambda shape: 1.0 + nrm(shape, 0.02)
    D = D_MODEL
    f_lin = jnp.linspace(C_FGATE_BIAS_LO, C_FGATE_BIAS_HI, C_HEADS, dtype=jnp.float32)
    zh = jnp.zeros((C_HEADS,), jnp.float32)
    gate_base = jnp.concatenate([zh, f_lin, zh, f_lin])
    return {
        'x_prompt': nrm((BATCH, SEQ, D), 1.0),
        'x_sample': nrm((DEC_BATCH, DEC_SEQ, D), 1.0),
        'c_prompt': nrm((BATCH, D), 1.0),
        'c_sample': nrm((DEC_BATCH, D), 1.0),
        'w_mod': nrm((DEPTH, D, 6 * D), 0.5 * D ** -0.5),
        'b_mod': nrm((DEPTH, 6 * D), 0.02),
        'norm1_g': gain((DEPTH, D)),
        'norm2_g': gain((DEPTH, D)),
        'even_w_in': nrm((N_EVEN, D, EVEN_IN), D ** -0.5),
        'even_lam_q1': nrm((N_EVEN, A_QK_DIM), 0.1),
        'even_lam_k1': nrm((N_EVEN, A_QK_DIM), 0.1),
        'even_lam_q2': nrm((N_EVEN, A_QK_DIM), 0.1),
        'even_lam_k2': nrm((N_EVEN, A_QK_DIM), 0.1),
        'even_attn_sub_g': gain((N_EVEN, A_V_DIM)),
        'even_gk_w_f': nrm((N_EVEN, B_GATE_RANK, B_HEADS * B_K_DIM), B_GATE_RANK ** -0.5),
        'even_gk_b_f': nrm((N_EVEN, B_HEADS * B_K_DIM), 0.1),
        'even_gk_w_b': nrm((N_EVEN, B_GATE_RANK, B_HEADS * B_K_DIM), B_GATE_RANK ** -0.5),
        'even_gk_b_b': nrm((N_EVEN, B_HEADS * B_K_DIM), 0.1),
        'even_gla_norm_g': gain((N_EVEN, B_HEADS * B_V_DIM)),
        'even_w_out': nrm((N_EVEN, EVEN_MIX, D), EVEN_MIX ** -0.5),
        'odd_w_in': nrm((N_ODD, D, ODD_IN), D ** -0.5),
        'odd_gate_b': gate_base[None, :] + nrm((N_ODD, 4 * C_HEADS), 0.1),
        'odd_norm_g': gain((N_ODD, ODD_MIX)),
        'odd_w_out': nrm((N_ODD, ODD_MIX, D), ODD_MIX ** -0.5),
        'ffn_w_gu': nrm((DEPTH, D, 2 * D_FF), D ** -0.5),
        'ffn_w_down': nrm((DEPTH, D_FF, D), D_FF ** -0.5),
        'final_g': gain((D,)),
    }


def reference(x_prompt, x_sample, c_prompt, c_sample, w_mod, b_mod, norm1_g, norm2_g, even_w_in,
              even_lam_q1, even_lam_k1, even_lam_q2, even_lam_k2, even_attn_sub_g, even_gk_w_f, even_gk_b_f,
              even_gk_w_b, even_gk_b_b, even_gla_norm_g, even_w_out, odd_w_in, odd_gate_b, odd_norm_g,
              odd_w_out, ffn_w_gu, ffn_w_down, final_g):
    y_prompt = trunk(x_prompt, c_prompt, w_mod, b_mod, norm1_g, norm2_g, even_w_in, even_lam_q1, even_lam_k1,
                     even_lam_q2, even_lam_k2, even_attn_sub_g, even_gk_w_f, even_gk_b_f, even_gk_w_b, even_gk_b_b,
                     even_gla_norm_g, even_w_out, odd_w_in, odd_gate_b, odd_norm_g, odd_w_out, ffn_w_gu,
                     ffn_w_down, final_g)
    y_sample = trunk(x_sample, c_sample, w_mod, b_mod, norm1_g, norm2_g, even_w_in, even_lam_q1, even_lam_k1,
                     even_lam_q2, even_lam_k2, even_attn_sub_g, even_gk_w_f, even_gk_b_f, even_gk_w_b, even_gk_b_b,
                     even_gla_norm_g, even_w_out, odd_w_in, odd_gate_b, odd_norm_g, odd_w_out, ffn_w_gu,
                     ffn_w_down, final_g)
    return (y_prompt, y_sample)
```

```python
import functools
import math

import jax
import jax.numpy as jnp
from jax import lax
from jax.experimental import pallas as pl
from jax.experimental.pallas import tpu as pltpu

F32 = jnp.float32
BF16 = jnp.bfloat16
HIGHEST = lax.Precision.HIGHEST

D_MODEL = 1024
DEPTH = 2
EPS = 1e-6
A_HEADS = 4
A_QK_DIM = 64
A_V_DIM = 128
B_HEADS = 4
B_K_DIM = 64
B_V_DIM = 128
B_GATE_RANK = 16
B_GATE_NORM = 16.0
B_CHUNK = 64
C_HEADS = 4
C_QK_DIM = 128
C_V_DIM = 256
C_CHUNK = 64
D_FF = 2816
EVEN_MAIN = 3072
ODD_MAIN = 3072
GATE_PAD = 128

VMEM_LIMIT = 56 * 1024 * 1024

_NT = (((1,), (1,)), ((), ()))
_TN = (((0,), (0,)), ((), ()))


def _log_sigmoid(x):
    return jnp.minimum(x, 0.0) - jnp.log1p(jnp.exp(-jnp.abs(x)))


def _silu(x):
    return x * jax.nn.sigmoid(x)


def _params(sem):
    return pltpu.CompilerParams(dimension_semantics=sem, vmem_limit_bytes=VMEM_LIMIT)


def _mod_kernel(c_ref, w_ref, b_ref, o_ref):
    s = _silu(c_ref[...])
    o_ref[0] = jnp.dot(s, w_ref[0], precision=HIGHEST, preferred_element_type=F32) + b_ref[0]


def _modulation(c_all, w_mod, b_mod):
    nb = c_all.shape[0]
    tn = 1536
    return pl.pallas_call(
        _mod_kernel,
        grid=(DEPTH, 6 * D_MODEL // tn),
        in_specs=[
            pl.BlockSpec((nb, D_MODEL), lambda i, j: (0, 0)),
            pl.BlockSpec((1, D_MODEL, tn), lambda i, j: (i, 0, j)),
            pl.BlockSpec((1, 1, tn), lambda i, j: (i, 0, j)),
        ],
        out_specs=pl.BlockSpec((1, nb, tn), lambda i, j: (i, 0, j)),
        out_shape=jax.ShapeDtypeStruct((DEPTH, nb, 6 * D_MODEL), F32),
        compiler_params=_params(("parallel", "parallel")),
        name="modulation",
    )(c_all, w_mod, b_mod.reshape(DEPTH, 1, 6 * D_MODEL))


def _norm_mod(x, g, sh, sc):
    ms = jnp.mean(x * x, axis=-1, keepdims=True)
    y = x * lax.rsqrt(ms + EPS) * g
    return y * (1.0 + sc) + sh


def _inproj_kernel(x_ref, sh_ref, sc_ref, g_ref, w_ref, wg_ref, *rest, n_main, n_col, chunk, with_rows):
    if with_rows:
        wgt_ref, z_ref, zg_ref, zgt_ref = rest
    else:
        z_ref, zg_ref = rest
    h = _norm_mod(x_ref[0], g_ref[...], sh_ref[0], sc_ref[0])
    hb = h.astype(BF16)
    for j in range(n_main // n_col):
        sl = slice(j * n_col, (j + 1) * n_col)
        z_ref[0, :, sl] = jnp.dot(hb, w_ref[:, sl], preferred_element_type=F32).astype(BF16)
    zg_ref[0] = jnp.dot(h, wg_ref[...], precision=HIGHEST, preferred_element_type=F32)
    if with_rows:
        tm = h.shape[0]
        for c in range(tm // chunk):
            hc = h[c * chunk:(c + 1) * chunk]
            zgt_ref[0, c] = lax.dot_general(wgt_ref[...], hc, _NT, precision=HIGHEST,
                                            preferred_element_type=F32)


def _inproj(x, sh, sc, g, w_main, w_gate, w_gate_t=None, *, tm=512, chunk=64):
    B, T, _ = x.shape
    n_main = w_main.shape[1]
    with_rows = w_gate_t is not None
    kern = functools.partial(_inproj_kernel, n_main=n_main, n_col=512, chunk=chunk, with_rows=with_rows)
    in_specs = [
        pl.BlockSpec((1, tm, D_MODEL), lambda b, i: (b, i, 0)),
        pl.BlockSpec((1, 1, D_MODEL), lambda b, i: (b, 0, 0)),
        pl.BlockSpec((1, 1, D_MODEL), lambda b, i: (b, 0, 0)),
        pl.BlockSpec((1, D_MODEL), lambda b, i: (0, 0)),
        pl.BlockSpec((D_MODEL, n_main), lambda b, i: (0, 0)),
        pl.BlockSpec((D_MODEL, GATE_PAD), lambda b, i: (0, 0)),
    ]
    out_specs = [
        pl.BlockSpec((1, tm, n_main), lambda b, i: (b, i, 0)),
        pl.BlockSpec((1, tm, GATE_PAD), lambda b, i: (b, i, 0)),
    ]
    out_shape = [
        jax.ShapeDtypeStruct((B, T, n_main), BF16),
        jax.ShapeDtypeStruct((B, T, GATE_PAD), F32),
    ]
    args = [x, sh, sc, g, w_main, w_gate]
    if with_rows:
        ng = w_gate_t.shape[0]
        in_specs.append(pl.BlockSpec((ng, D_MODEL), lambda b, i: (0, 0)))
        out_specs.append(pl.BlockSpec((1, tm // chunk, ng, chunk), lambda b, i: (b, i, 0, 0)))
        out_shape.append(jax.ShapeDtypeStruct((B, T // chunk, ng, chunk), F32))
        args.append(w_gate_t)
    return pl.pallas_call(
        kern,
        grid=(B, T // tm),
        in_specs=in_specs,
        out_specs=out_specs,
        out_shape=out_shape,
        compiler_params=_params(("parallel", "parallel")),
        name="inproj_rows" if with_rows else "inproj",
    )(*args)


def _attn_kernel(slope_ref, q_ref, k_ref, v_ref, lamp_ref, subg_ref, o_ref, m_scr, l_scr, acc_scr,
                 *, tq, tk, lam_init):
    h = pl.program_id(1)
    qi = pl.program_id(2)
    ki = pl.program_id(3)
    nk = pl.num_programs(3)

    @pl.when(ki == 0)
    def _():
        m_scr[...] = jnp.full(m_scr.shape, -jnp.inf, F32)
        l_scr[...] = jnp.zeros(l_scr.shape, F32)
        acc_scr[...] = jnp.zeros(acc_scr.shape, F32)

    q = q_ref[0]
    k = k_ref[0]
    v = v_ref[0]
    lane = lax.broadcasted_iota(jnp.int32, (1, 2 * A_QK_DIM), 1)
    scale = jnp.asarray(A_QK_DIM ** -0.5, BF16)
    zero = jnp.zeros((), BF16)
    qs = q * scale
    q_halves = (jnp.where(lane < A_QK_DIM, qs, zero), jnp.where(lane >= A_QK_DIM, qs, zero))
    rows = qi * tq + lax.broadcasted_iota(jnp.int32, (tq, tk), 0)
    cols = ki * tk + lax.broadcasted_iota(jnp.int32, (tq, tk), 1)
    bias = -slope_ref[h] * jnp.abs(rows - cols).astype(F32)
    for c in range(2):
        s = lax.dot_general(q_halves[c], k, _NT, preferred_element_type=F32) + bias
        m_prev = m_scr[c]
        m_new = jnp.maximum(m_prev, jnp.max(s, axis=-1, keepdims=True))
        alpha = jnp.exp(m_prev - m_new)
        p = jnp.exp(s - m_new)
        l_scr[c] = alpha * l_scr[c] + jnp.sum(p, axis=-1, keepdims=True)
        acc_scr[c] = alpha * acc_scr[c] + jnp.dot(p.astype(BF16), v, preferred_element_type=F32)
        m_scr[c] = m_new

    @pl.when(ki == nk - 1)
    def _():
        lp = lamp_ref[...]
        lam = (jnp.exp(jnp.sum(lp[0:1] * lp[1:2], axis=-1, keepdims=True))
               - jnp.exp(jnp.sum(lp[2:3] * lp[3:4], axis=-1, keepdims=True)) + lam_init)
        o = acc_scr[0] / l_scr[0] - lam * (acc_scr[1] / l_scr[1])
        ms = jnp.mean(o * o, axis=-1, keepdims=True)
        y = o * lax.rsqrt(ms + EPS) * subg_ref[...]
        o_ref[0] = (y * (1.0 - lam_init)).astype(o_ref.dtype)


def _diff_attention(z, lam_params, sub_g, lam_init, *, tq=256, tk=512):
    B, T, _ = z.shape
    slopes = jnp.exp2(-8.0 * jnp.arange(1, A_HEADS + 1, dtype=F32) / A_HEADS)
    kern = functools.partial(_attn_kernel, tq=tq, tk=tk, lam_init=lam_init)
    hw = 2 * A_QK_DIM
    return pl.pallas_call(
        kern,
        grid=(B, A_HEADS, T // tq, T // tk),
        in_specs=[
            pl.BlockSpec(memory_space=pltpu.SMEM),
            pl.BlockSpec((1, tq, hw), lambda b, h, i, j: (b, i, h)),
            pl.BlockSpec((1, tk, hw), lambda b, h, i, j: (b, j, A_HEADS + h)),
            pl.BlockSpec((1, tk, A_V_DIM), lambda b, h, i, j: (b, j, 2 * A_HEADS + h)),
            pl.BlockSpec((4, A_QK_DIM), lambda b, h, i, j: (0, 0)),
            pl.BlockSpec((1, A_V_DIM), lambda b, h, i, j: (0, 0)),
        ],
        out_specs=pl.BlockSpec((1, tq, A_V_DIM), lambda b, h, i, j: (b, i, h)),
        out_shape=jax.ShapeDtypeStruct((B, T, A_HEADS * A_V_DIM), BF16),
        scratch_shapes=[
            pltpu.VMEM((2, tq, 1), F32),
            pltpu.VMEM((2, tq, 1), F32),
            pltpu.VMEM((2, tq, A_V_DIM), F32),
        ],
        compiler_params=_params(("parallel", "parallel", "parallel", "arbitrary")),
        name="diff_attention",
    )(slopes, z, z, z, lam_params, sub_g)


def _gla_kernel(qf_ref, kf_ref, vf_ref, lrf_ref, qb_ref, kb_ref, vb_ref, lrb_ref, gw_ref, gb_ref,
                of_ref, ob_ref, s_scr, *, tg):
    L = B_CHUNK
    nchunk = tg // L
    pw = 2 * B_K_DIM
    vw = 2 * B_V_DIM

    @pl.when(pl.program_id(1) == 0)
    def _():
        s_scr[...] = jnp.zeros(s_scr.shape, F32)

    tt = lax.broadcasted_iota(jnp.int32, (L, L), 0)
    ss = lax.broadcasted_iota(jnp.int32, (L, L), 1)
    lane = lax.broadcasted_iota(jnp.int32, (1, pw), 1)
    srow = lax.broadcasted_iota(jnp.int32, (vw, pw), 0)
    slane = lax.broadcasted_iota(jnp.int32, (vw, pw), 1)
    same_head = (srow >= B_V_DIM) == (slane >= B_K_DIM)
    zero_b = jnp.zeros((), BF16)

    for d, (q_ref, k_ref, v_ref, lr_ref, o_ref) in enumerate(
            ((qf_ref, kf_ref, vf_ref, lrf_ref, of_ref), (qb_ref, kb_ref, vb_ref, lrb_ref, ob_ref))):
        keep = (ss <= tt) if d == 0 else (ss >= tt)
        tri = keep.astype(F32)
        pre = jnp.dot(lr_ref[0], gw_ref[d], precision=HIGHEST, preferred_element_type=F32) + gb_ref[d]
        lg = _log_sigmoid(pre) * (1.0 / B_GATE_NORM)
        for ci in range(nchunk):
            c = ci if d == 0 else nchunk - 1 - ci
            sl = slice(c * L, (c + 1) * L)
            b = jnp.dot(tri, lg[sl], precision=HIGHEST, preferred_element_type=F32)
            b_end = b[L - 1:L] if d == 0 else b[0:1]
            qc = q_ref[0, sl, :].astype(F32)
            kc = k_ref[0, sl, :].astype(F32)
            qe = (qc * (B_K_DIM ** -0.5) * jnp.exp(b)).astype(BF16)
            ke = (kc * jnp.exp(-b)).astype(BF16)
            kd = (kc * jnp.exp(b_end - b)).astype(BF16)
            dec = jnp.exp(b_end)
            for p in range(B_HEADS // 2):
                ksl = slice(p * pw, (p + 1) * pw)
                qe_p, ke_p, kd_p = qe[:, ksl], ke[:, ksl], kd[:, ksl]
                v_p = v_ref[0, sl, p * vw:(p + 1) * vw]
                intra = []
                for hh in range(2):
                    sel = (lane < B_K_DIM) if hh == 0 else (lane >= B_K_DIM)
                    att = lax.dot_general(jnp.where(sel, qe_p, zero_b), ke_p, _NT,
                                          preferred_element_type=F32)
                    att = jnp.where(keep, att, 0.0).astype(BF16)
                    intra.append(jnp.dot(att, v_p[:, hh * B_V_DIM:(hh + 1) * B_V_DIM],
                                         preferred_element_type=F32))
                st = s_scr[d, p]
                inter = lax.dot_general(qe_p, st.astype(BF16), _NT, preferred_element_type=F32)
                o_ref[0, sl, p * vw:(p + 1) * vw] = inter + jnp.concatenate(intra, axis=-1)
                ut = lax.dot_general(v_p, kd_p, _TN, preferred_element_type=F32)
                s_scr[d, p] = dec[:, ksl] * st + jnp.where(same_head, ut, 0.0)


def _gla(z, zlr, gw, gb, *, tg=512):
    B, T, _ = z.shape
    nb = T // tg
    kw = B_HEADS * B_K_DIM
    vw = B_HEADS * B_V_DIM
    q_blk = (A_HEADS * 2 * A_QK_DIM * 2 + A_HEADS * A_V_DIM) // kw
    v_blk = (1536 + 2 * kw) // vw
    fwd = lambda b, i: (b, i)
    bwd = lambda b, i: (b, nb - 1 - i)

    def specs(order):
        return [
            pl.BlockSpec((1, tg, kw), lambda b, i: (*order(b, i), q_blk)),
            pl.BlockSpec((1, tg, kw), lambda b, i: (*order(b, i), q_blk + 1)),
            pl.BlockSpec((1, tg, vw), lambda b, i: (*order(b, i), v_blk)),
            pl.BlockSpec((1, tg, GATE_PAD), lambda b, i: (*order(b, i), 0)),
        ]

    kern = functools.partial(_gla_kernel, tg=tg)
    return pl.pallas_call(
        kern,
        grid=(B, nb),
        in_specs=specs(fwd) + specs(bwd) + [
            pl.BlockSpec((2, GATE_PAD, kw), lambda b, i: (0, 0, 0)),
            pl.BlockSpec((2, 1, kw), lambda b, i: (0, 0, 0)),
        ],
        out_specs=[
            pl.BlockSpec((1, tg, vw), lambda b, i: (b, i, 0)),
            pl.BlockSpec((1, tg, vw), lambda b, i: (b, nb - 1 - i, 0)),
        ],
        out_shape=[jax.ShapeDtypeStruct((B, T, vw), F32)] * 2,
        scratch_shapes=[pltpu.VMEM((2, B_HEADS // 2, 2 * B_V_DIM, 2 * B_K_DIM), F32)],
        compiler_params=_params(("parallel", "arbitrary")),
        name="gla",
    )(z, z, z, zlr, z, z, z, zlr, gw, gb)


def _head_rmsnorm(o, g, width):
    parts = []
    for h in range(o.shape[-1] // width):
        oh = o[:, h * width:(h + 1) * width]
        ms = jnp.mean(oh * oh, axis=-1, keepdims=True)
        parts.append(oh * lax.rsqrt(ms + EPS) * g[:, h * width:(h + 1) * width])
    return jnp.concatenate(parts, axis=-1)


def _outproj_even_kernel(a_ref, of_ref, ob_ref, bg_ref, x_ref, g1_ref, gn_ref, w_ref, o_ref):
    na = A_HEADS * A_V_DIM
    o = _head_rmsnorm(of_ref[0] + ob_ref[0], gn_ref[...], B_V_DIM)
    b_out = (o * _silu(bg_ref[0].astype(F32))).astype(BF16)
    res = jnp.dot(a_ref[0], w_ref[0:na, :], preferred_element_type=F32)
    res = res + jnp.dot(b_out, w_ref[na:, :], preferred_element_type=F32)
    o_ref[0] = x_ref[0] + g1_ref[0] * res


def _outproj_even(a, of, ob, z, x, g1, gn, w, *, tm=512):
    B, T, _ = x.shape
    nm = a.shape[-1]
    bg_blk = (1536 + 512 + 512) // nm
    tok = lambda b, i: (b, i, 0)
    return pl.pallas_call(
        _outproj_even_kernel,
        grid=(B, T // tm),
        in_specs=[
            pl.BlockSpec((1, tm, nm), tok),
            pl.BlockSpec((1, tm, nm), tok),
            pl.BlockSpec((1, tm, nm), tok),
            pl.BlockSpec((1, tm, nm), lambda b, i: (b, i, bg_blk)),
            pl.BlockSpec((1, tm, D_MODEL), tok),
            pl.BlockSpec((1, 1, D_MODEL), lambda b, i: (b, 0, 0)),
            pl.BlockSpec((1, nm), lambda b, i: (0, 0)),
            pl.BlockSpec((2 * nm, D_MODEL), lambda b, i: (0, 0)),
        ],
        out_specs=pl.BlockSpec((1, tm, D_MODEL), tok),
        out_shape=jax.ShapeDtypeStruct((B, T, D_MODEL), F32),
        compiler_params=_params(("parallel", "parallel")),
        name="outproj_even",
    )(a, of, ob, z, x, g1, gn, w)


def _mlstm_kernel(gb_ref, qf_ref, kf_ref, vf_ref, gcf_ref, grf_ref, qb_ref, kb_ref, vb_ref, gcb_ref, grb_ref,
                  of_ref, ob_ref, c_scr, n_scr, m_scr, *, tg):
    L = C_CHUNK
    nchunk = tg // L
    dk, dv = C_QK_DIM, C_V_DIM

    @pl.when(pl.program_id(1) == 0)
    def _():
        c_scr[...] = jnp.zeros(c_scr.shape, F32)
        n_scr[...] = jnp.zeros(n_scr.shape, F32)
        m_scr[...] = jnp.zeros(m_scr.shape, F32)

    tt = lax.broadcasted_iota(jnp.int32, (L, L), 0)
    ss = lax.broadcasted_iota(jnp.int32, (L, L), 1)

    for d, (q_ref, k_ref, v_ref, gc_ref, gr_ref, o_ref) in enumerate(
            ((qf_ref, kf_ref, vf_ref, gcf_ref, grf_ref, of_ref),
             (qb_ref, kb_ref, vb_ref, gcb_ref, grb_ref, ob_ref))):
        keep = (ss <= tt) if d == 0 else (ss >= tt)
        keep_t = (tt <= ss) if d == 0 else (tt >= ss)
        for ci in range(nchunk):
            c = ci if d == 0 else nchunk - 1 - ci
            sl = slice(c * L, (c + 1) * L)
            gcol = gc_ref[0, sl, :]
            grow = gr_ref[0, c]
            for h in range(C_HEADS):
                ii = (2 * d) * C_HEADS + h
                fi = (2 * d + 1) * C_HEADS + h
                i_row = grow[ii:ii + 1, :] + gb_ref[ii]
                f_row = _log_sigmoid(grow[fi:fi + 1, :] + gb_ref[fi])
                i_col = gcol[:, ii:ii + 1] + gb_ref[ii]
                f_col = _log_sigmoid(gcol[:, fi:fi + 1] + gb_ref[fi])
                F_col = jnp.sum(jnp.where(keep, f_row, 0.0), axis=1, keepdims=True)
                F_row = jnp.sum(jnp.where(keep_t, f_col, 0.0), axis=0, keepdims=True)
                Dm = jnp.where(keep, F_col - F_row + i_row, -jnp.inf)
                m_old = m_scr[d, h]
                a_col = F_col + m_old
                mt = jnp.maximum(a_col, jnp.max(Dm, axis=1, keepdims=True))
                qc = q_ref[0, sl, h * dk:(h + 1) * dk]
                kf32 = k_ref[0, sl, h * dk:(h + 1) * dk].astype(F32) * (dk ** -0.5)
                kc = kf32.astype(BF16)
                vc = v_ref[0, sl, h * dv:(h + 1) * dv]
                W = jnp.exp(Dm - mt) * lax.dot_general(qc, kc, _NT, preferred_element_type=F32)
                inter = jnp.exp(a_col - mt)
                C = c_scr[d, h]
                n = n_scr[d, h]
                num = inter * jnp.dot(qc, C.astype(BF16), preferred_element_type=F32)
                num = num + jnp.dot(W.astype(BF16), vc, preferred_element_type=F32)
                qn = jnp.sum(qc.astype(F32) * n, axis=1, keepdims=True)
                den = inter * qn + jnp.sum(W, axis=1, keepdims=True)
                o_ref[0, sl, h * dv:(h + 1) * dv] = num / jnp.maximum(jnp.abs(den), jnp.exp(-mt))
                FL = F_row[:, L - 1:L] if d == 0 else F_row[:, 0:1]
                wl_row = FL - F_row + i_row
                wl_col = FL - F_col + i_col
                m_new = jnp.maximum(FL + m_old, jnp.max(wl_row, axis=1, keepdims=True))
                dec = jnp.exp(FL + m_old - m_new)
                kw = kf32 * jnp.exp(wl_col - m_new)
                c_scr[d, h] = dec * C + lax.dot_general(kw.astype(BF16), vc, _TN, preferred_element_type=F32)
                n_scr[d, h] = dec * n + jnp.sum(kw, axis=0, keepdims=True)
                m_scr[d, h] = m_new


def _mlstm(z, gcol, grow, gate_b, *, tg=256):
    B, T, _ = z.shape
    nb = T // tg
    qw = C_HEADS * C_QK_DIM
    vw = C_HEADS * C_V_DIM
    ng = grow.shape[2]
    fwd = lambda b, i: (b, i)
    bwd = lambda b, i: (b, nb - 1 - i)

    def specs(order):
        return [
            pl.BlockSpec((1, tg, qw), lambda b, i: (*order(b, i), 0)),
            pl.BlockSpec((1, tg, qw), lambda b, i: (*order(b, i), 1)),
            pl.BlockSpec((1, tg, vw), lambda b, i: (*order(b, i), 1)),
            pl.BlockSpec((1, tg, GATE_PAD), lambda b, i: (*order(b, i), 0)),
            pl.BlockSpec((1, tg // C_CHUNK, ng, C_CHUNK), lambda b, i: (*order(b, i), 0, 0)),
        ]

    kern = functools.partial(_mlstm_kernel, tg=tg)
    return pl.pallas_call(
        kern,
        grid=(B, nb),
        in_specs=[pl.BlockSpec(memory_space=pltpu.SMEM)] + specs(fwd) + specs(bwd),
        out_specs=[
            pl.BlockSpec((1, tg, vw), lambda b, i: (b, i, 0)),
            pl.BlockSpec((1, tg, vw), lambda b, i: (b, nb - 1 - i, 0)),
        ],
        out_shape=[jax.ShapeDtypeStruct((B, T, vw), F32)] * 2,
        scratch_shapes=[
            pltpu.VMEM((2, C_HEADS, C_QK_DIM, C_V_DIM), F32),
            pltpu.VMEM((2, C_HEADS, 1, C_QK_DIM), F32),
            pltpu.VMEM((2, C_HEADS, 1, 1), F32),
        ],
        compiler_params=_params(("parallel", "arbitrary")),
        name="mlstm",
    )(gate_b, z, z, z, gcol, grow, z, z, z, gcol, grow)


def _outproj_odd_kernel(hf_ref, hb_ref, co_ref, x_ref, g1_ref, gn_ref, w_ref, o_ref):
    ht = _head_rmsnorm(hf_ref[0] + hb_ref[0], gn_ref[...], C_V_DIM)
    mix = (jax.nn.sigmoid(co_ref[0].astype(F32)) * ht).astype(BF16)
    o_ref[0] = x_ref[0] + g1_ref[0] * jnp.dot(mix, w_ref[...], preferred_element_type=F32)


def _outproj_odd(hf, hb, z, x, g1, gn, w, *, tm=512):
    B, T, _ = x.shape
    nm = hf.shape[-1]
    tok = lambda b, i: (b, i, 0)
    return pl.pallas_call(
        _outproj_odd_kernel,
        grid=(B, T // tm),
        in_specs=[
            pl.BlockSpec((1, tm, nm), tok),
            pl.BlockSpec((1, tm, nm), tok),
            pl.BlockSpec((1, tm, nm), lambda b, i: (b, i, 2)),
            pl.BlockSpec((1, tm, D_MODEL), tok),
            pl.BlockSpec((1, 1, D_MODEL), lambda b, i: (b, 0, 0)),
            pl.BlockSpec((1, nm), lambda b, i: (0, 0)),
            pl.BlockSpec((nm, D_MODEL), lambda b, i: (0, 0)),
        ],
        out_specs=pl.BlockSpec((1, tm, D_MODEL), tok),
        out_shape=jax.ShapeDtypeStruct((B, T, D_MODEL), F32),
        compiler_params=_params(("parallel", "parallel")),
        name="outproj_odd",
    )(hf, hb, z, x, g1, gn, w)


def _ffn_kernel(x_ref, sh_ref, sc_ref, g_ref, g2_ref, wg_ref, wu_ref, wd_ref, fg_ref, o_ref, h_scr, acc_scr,
                *, final_norm):
    j = pl.program_id(2)

    @pl.when(j == 0)
    def _():
        h_scr[...] = _norm_mod(x_ref[0], g_ref[...], sh_ref[0], sc_ref[0]).astype(BF16)
        acc_scr[...] = jnp.zeros(acc_scr.shape, F32)

    hb = h_scr[...]
    gate = jnp.dot(hb, wg_ref[...], preferred_element_type=F32)
    up = jnp.dot(hb, wu_ref[...], preferred_element_type=F32)
    act = (_silu(gate) * up).astype(BF16)
    acc_scr[...] += jnp.dot(act, wd_ref[...], preferred_element_type=F32)

    @pl.when(j == pl.num_programs(2) - 1)
    def _():
        y = x_ref[0] + g2_ref[0] * acc_scr[...]
        if final_norm:
            ms = jnp.mean(y * y, axis=-1, keepdims=True)
            y = y * lax.rsqrt(ms + EPS) * fg_ref[...]
        o_ref[0] = y


def _ffn(x, sh, sc, g, g2, w_gu, w_down, final_g, *, final_norm, tm=512, tf=1408):
    B, T, _ = x.shape
    nf = D_FF // tf
    tok = lambda b, i, j: (b, i, 0)
    per_b = lambda b, i, j: (b, 0, 0)
    const = lambda b, i, j: (0, 0)
    kern = functools.partial(_ffn_kernel, final_norm=final_norm)
    return pl.pallas_call(
        kern,
        grid=(B, T // tm, nf),
        in_specs=[
            pl.BlockSpec((1, tm, D_MODEL), tok),
            pl.BlockSpec((1, 1, D_MODEL), per_b),
            pl.BlockSpec((1, 1, D_MODEL), per_b),
            pl.BlockSpec((1, D_MODEL), const),
            pl.BlockSpec((1, 1, D_MODEL), per_b),
            pl.BlockSpec((D_MODEL, tf), lambda b, i, j: (0, j)),
            pl.BlockSpec((D_MODEL, tf), lambda b, i, j: (0, nf + j)),
            pl.BlockSpec((tf, D_MODEL), lambda b, i, j: (j, 0)),
            pl.BlockSpec((1, D_MODEL), const),
        ],
        out_specs=pl.BlockSpec((1, tm, D_MODEL), tok),
        out_shape=jax.ShapeDtypeStruct((B, T, D_MODEL), F32),
        scratch_shapes=[pltpu.VMEM((tm, D_MODEL), BF16), pltpu.VMEM((tm, D_MODEL), F32)],
        compiler_params=_params(("parallel", "parallel", "arbitrary")),
        name="ffn_final" if final_norm else "ffn",
    )(x, sh, sc, g, g2, w_gu, w_gu, w_down, final_g)


def _pad_cols(w, n):
    return jnp.pad(w, ((0, 0), (0, n - w.shape[1])))


def _trunk(x, mod, p):
    for i in range(DEPTH):
        sh1, sc1, g1, sh2, sc2, g2 = [m[:, None, :] for m in jnp.split(mod[i], 6, axis=-1)]
        if i % 2 == 0:
            j = i // 2
            lam_init = 0.8 - 0.6 * math.exp(-0.3 * i)
            z, zlr = _inproj(x, sh1, sc1, p["norm1_g"][i], p["even_w_main"][j], p["even_w_lr"][j])
            a = _diff_attention(z, p["even_lam"][j], p["even_sub_g"][j], lam_init)
            of, ob = _gla(z, zlr, p["even_gk_w"][j], p["even_gk_b"][j])
            x = _outproj_even(a, of, ob, z, x, g1, p["even_gla_norm_g"][j], p["even_w_out"][j])
        else:
            j = i // 2
            z, gcol, grow = _inproj(x, sh1, sc1, p["norm1_g"][i], p["odd_w_main"][j], p["odd_w_gate"][j],
                                    p["odd_w_gate_t"][j], chunk=C_CHUNK)
            hf, hb = _mlstm(z, gcol, grow, p["odd_gate_b"][j])
            x = _outproj_odd(hf, hb, z, x, g1, p["odd_norm_g"][j], p["odd_w_out"][j])
        x = _ffn(x, sh2, sc2, p["norm2_g"][i], g2, p["ffn_w_gu"][i], p["ffn_w_down"][i], p["final_g"],
                 final_norm=(i == DEPTH - 1))
    return x


def kernel(x_prompt, x_sample, c_prompt, c_sample, w_mod, b_mod, norm1_g, norm2_g, even_w_in, even_lam_q1, even_lam_k1, even_lam_q2, even_lam_k2, even_attn_sub_g, even_gk_w_f, even_gk_b_f, even_gk_w_b, even_gk_b_b, even_gla_norm_g, even_w_out, odd_w_in, odd_gate_b, odd_norm_g, odd_w_out, ffn_w_gu, ffn_w_down, final_g):
    n_even = even_w_in.shape[0]
    n_odd = odd_w_in.shape[0]
    r = B_GATE_RANK
    kw = B_HEADS * B_K_DIM
    gk_w = jnp.zeros((n_even, 2, GATE_PAD, kw), F32)
    gk_w = gk_w.at[:, 0, 0:r].set(even_gk_w_f).at[:, 1, r:2 * r].set(even_gk_w_b)
    p = {
        "norm1_g": norm1_g[:, None, :],
        "norm2_g": norm2_g[:, None, :],
        "final_g": final_g[None, :],
        "even_w_main": even_w_in[:, :, :EVEN_MAIN].astype(BF16),
        "even_w_lr": jnp.stack([_pad_cols(even_w_in[j, :, EVEN_MAIN:], GATE_PAD) for j in range(n_even)]),
        "even_lam": jnp.stack([even_lam_q1, even_lam_k1, even_lam_q2, even_lam_k2], axis=1),
        "even_sub_g": even_attn_sub_g[:, None, :],
        "even_gk_w": gk_w,
        "even_gk_b": jnp.stack([even_gk_b_f, even_gk_b_b], axis=1)[:, :, None, :],
        "even_gla_norm_g": even_gla_norm_g[:, None, :],
        "even_w_out": even_w_out.astype(BF16),
        "odd_w_main": odd_w_in[:, :, :ODD_MAIN].astype(BF16),
        "odd_w_gate": jnp.stack([_pad_cols(odd_w_in[j, :, ODD_MAIN:], GATE_PAD) for j in range(n_odd)]),
        "odd_w_gate_t": jnp.swapaxes(odd_w_in[:, :, ODD_MAIN:], 1, 2),
        "odd_gate_b": odd_gate_b,
        "odd_norm_g": odd_norm_g[:, None, :],
        "odd_w_out": odd_w_out.astype(BF16),
        "ffn_w_gu": ffn_w_gu.astype(BF16),
        "ffn_w_down": ffn_w_down.astype(BF16),
    }
    nbp = c_prompt.shape[0]
    mod = _modulation(jnp.concatenate([c_prompt, c_sample], axis=0), w_mod, b_mod)
    y_prompt = _trunk(x_prompt, mod[:, :nbp], p)
    y_sample = _trunk(x_sample, mod[:, nbp:], p)
    return (y_prompt, y_sample)
```

```python
import functools
import math

import jax
import jax.numpy as jnp
from jax import lax
from jax.experimental import pallas as pl
from jax.experimental.pallas import tpu as pltpu

F32 = jnp.float32
BF16 = jnp.bfloat16
HIGHEST = lax.Precision.HIGHEST

D_MODEL = 1024
DEPTH = 2
EPS = 1e-6
A_HEADS = 4
A_QK_DIM = 64
A_V_DIM = 128
B_HEADS = 4
B_K_DIM = 64
B_V_DIM = 128
B_GATE_RANK = 16
B_GATE_NORM = 16.0
B_CHUNK = 64
C_HEADS = 4
C_QK_DIM = 128
C_V_DIM = 256
C_CHUNK = 64
D_FF = 2816
EVEN_AV = 1024
EVEN_REST = 1536
EVEN_MAIN = 3072
EVEN_BQ = 1024
EVEN_BV = 1536
EVEN_BG = 2048
ODD_MAIN = 3072
GATE_PAD = 128
ONES_ROWS = 16
LOG2E = 1.4426950408889634

VMEM_LIMIT = 56 * 1024 * 1024

_NT = (((1,), (1,)), ((), ()))
_TN = (((0,), (0,)), ((), ()))


def _log_sigmoid(x):
    return jnp.minimum(x, 0.0) - jnp.log1p(jnp.exp(-jnp.abs(x)))


def _silu(x):
    return x * jax.nn.sigmoid(x)


def _split_bf16(x, terms):
    parts = []
    for _ in range(terms - 1):
        hi = x.astype(BF16)
        parts.append(hi)
        x = x - hi.astype(F32)
    parts.append(x.astype(BF16))
    return parts


def _dot_exact_lhs(a_bf16, x):
    return sum(jnp.dot(a_bf16, part, preferred_element_type=F32) for part in _split_bf16(x, 3))


def _dot_exact_rhs(x, a_bf16):
    return sum(jnp.dot(part, a_bf16, preferred_element_type=F32) for part in _split_bf16(x, 3))


def _dot_split2(x, w):
    xh, xl = _split_bf16(x, 2)
    wh, wl = _split_bf16(w, 2)
    return (jnp.dot(xh, wh, preferred_element_type=F32) + jnp.dot(xh, wl, preferred_element_type=F32)
            + jnp.dot(xl, wh, preferred_element_type=F32))


def _params(sem):
    return pltpu.CompilerParams(dimension_semantics=sem, vmem_limit_bytes=VMEM_LIMIT)


def _mod_kernel(c_ref, w_ref, b_ref, o_ref):
    s = _silu(c_ref[...])
    o_ref[0] = jnp.dot(s, w_ref[0], precision=HIGHEST, preferred_element_type=F32) + b_ref[0]


def _modulation(c_all, w_mod, b_mod):
    nb = c_all.shape[0]
    tn = 1536
    return pl.pallas_call(
        _mod_kernel,
        grid=(DEPTH, 6 * D_MODEL // tn),
        in_specs=[
            pl.BlockSpec((nb, D_MODEL), lambda i, j: (0, 0)),
            pl.BlockSpec((1, D_MODEL, tn), lambda i, j: (i, 0, j)),
            pl.BlockSpec((1, 1, tn), lambda i, j: (i, 0, j)),
        ],
        out_specs=pl.BlockSpec((1, nb, tn), lambda i, j: (i, 0, j)),
        out_shape=jax.ShapeDtypeStruct((DEPTH, nb, 6 * D_MODEL), F32),
        compiler_params=_params(("parallel", "parallel")),
        name="modulation",
    )(c_all, w_mod, b_mod.reshape(DEPTH, 1, 6 * D_MODEL))


def _norm_mod(x, g, sh, sc):
    ms = jnp.mean(x * x, axis=-1, keepdims=True)
    y = x * lax.rsqrt(ms + EPS) * g
    return y * (1.0 + sc) + sh


def _inproj_kernel(x_ref, sh_ref, sc_ref, g_ref, w_ref, wg_ref, wt_ref, z_ref, zg_ref, zt_ref,
                   *, n_main, n_col, chunk, with_rows):
    h = _norm_mod(x_ref[0], g_ref[...], sh_ref[0], sc_ref[0])
    hb = h.astype(BF16)
    for j in range(n_main // n_col):
        sl = slice(j * n_col, (j + 1) * n_col)
        z_ref[0, :, sl] = jnp.dot(hb, w_ref[:, sl], preferred_element_type=F32).astype(BF16)
    zg_ref[0] = jnp.dot(h, wg_ref[...], precision=HIGHEST, preferred_element_type=F32)
    if with_rows:
        tm = h.shape[0]
        for c in range(tm // chunk):
            hc = h[c * chunk:(c + 1) * chunk]
            zt_ref[0, c] = lax.dot_general(wt_ref[...], hc, _NT, precision=HIGHEST,
                                           preferred_element_type=F32)
    else:
        zt_ref[0] = lax.dot_general(wt_ref[...], hb, _NT, preferred_element_type=F32).astype(BF16)


def _inproj(x, sh, sc, g, w_main, w_gate, w_t, *, with_rows, tm=512, chunk=64):
    B, T, _ = x.shape
    n_main = w_main.shape[1]
    kern = functools.partial(_inproj_kernel, n_main=n_main, n_col=512, chunk=chunk, with_rows=with_rows)
    in_specs = [
        pl.BlockSpec((1, tm, D_MODEL), lambda b, i: (b, i, 0)),
        pl.BlockSpec((1, 1, D_MODEL), lambda b, i: (b, 0, 0)),
        pl.BlockSpec((1, 1, D_MODEL), lambda b, i: (b, 0, 0)),
        pl.BlockSpec((1, D_MODEL), lambda b, i: (0, 0)),
        pl.BlockSpec((D_MODEL, n_main), lambda b, i: (0, 0)),
        pl.BlockSpec((D_MODEL, GATE_PAD), lambda b, i: (0, 0)),
    ]
    out_specs = [
        pl.BlockSpec((1, tm, n_main), lambda b, i: (b, i, 0)),
        pl.BlockSpec((1, tm, GATE_PAD), lambda b, i: (b, i, 0)),
    ]
    out_shape = [
        jax.ShapeDtypeStruct((B, T, n_main), BF16),
        jax.ShapeDtypeStruct((B, T, GATE_PAD), F32),
    ]
    args = [x, sh, sc, g, w_main, w_gate, w_t]
    nt = w_t.shape[0]
    in_specs.append(pl.BlockSpec((nt, D_MODEL), lambda b, i: (0, 0)))
    if with_rows:
        out_specs.append(pl.BlockSpec((1, tm // chunk, nt, chunk), lambda b, i: (b, i, 0, 0)))
        out_shape.append(jax.ShapeDtypeStruct((B, T // chunk, nt, chunk), F32))
    else:
        out_specs.append(pl.BlockSpec((1, nt, tm), lambda b, i: (b, 0, i)))
        out_shape.append(jax.ShapeDtypeStruct((B, nt, T), BF16))
    return pl.pallas_call(
        kern,
        grid=(B, T // tm),
        in_specs=in_specs,
        out_specs=out_specs,
        out_shape=out_shape,
        compiler_params=_params(("parallel", "parallel")),
        name="inproj_rows" if with_rows else "inproj",
    )(*args)


def _attn_kernel(slope_ref, q_ref, k_ref, vt_ref, lamp_ref, subg_ref, o_ref, m_scr, acc_scr,
                 *, tq, tk, sk, lam_init):
    h = pl.program_id(1)
    qi = pl.program_id(2)
    ki = pl.program_id(3)
    nk = pl.num_programs(3)
    nsub = tk // sk
    dv = A_V_DIM

    @pl.when(ki == 0)
    def _():
        m_scr[...] = jnp.full(m_scr.shape, -jnp.inf, F32)
        acc_scr[...] = jnp.zeros(acc_scr.shape, F32)

    m = m_scr[...]
    acc = acc_scr[...]
    lane = lax.broadcasted_iota(jnp.int32, (1, 2 * A_QK_DIM), 1)
    qs = (q_ref[0].astype(F32) * (A_QK_DIM ** -0.5 * LOG2E)).astype(BF16)
    zero = jnp.zeros((), BF16)
    q_both = jnp.concatenate([jnp.where(lane < A_QK_DIM, qs, zero), jnp.where(lane >= A_QK_DIM, qs, zero)],
                             axis=0)
    rel = (lax.broadcasted_iota(jnp.int32, (sk, tq), 0)
           - lax.broadcasted_iota(jnp.int32, (sk, tq), 1)).astype(F32)
    nslope = -slope_ref[h] * LOG2E
    ones = jnp.ones((ONES_ROWS, sk), BF16)

    def scores(j):
        delta = (ki * tk + j * sk - qi * tq).astype(F32)
        bias = nslope * jnp.abs(rel + delta)
        s = lax.dot_general(k_ref[0, j * sk:(j + 1) * sk, :], q_both, _NT, preferred_element_type=F32)
        s = s + jnp.concatenate([bias, bias], axis=1)
        return s, jnp.max(s, axis=0, keepdims=True)

    nxt = scores(0)
    for j in range(nsub):
        s, m_loc = nxt
        if j + 1 < nsub:
            nxt = scores(j + 1)
        m_new = jnp.maximum(m, m_loc)
        alpha = jnp.exp2(m - m_new)
        p = jnp.exp2(s - m_new).astype(BF16)
        vt1 = jnp.concatenate([vt_ref[0, :, j * sk:(j + 1) * sk], ones], axis=0)
        acc = alpha * acc + jnp.dot(vt1, p, preferred_element_type=F32)
        m = m_new
    m_scr[...] = m
    acc_scr[...] = acc

    @pl.when(ki == nk - 1)
    def _():
        lp = lamp_ref[...]
        lam = (jnp.exp(jnp.sum(lp[0:1] * lp[1:2], axis=-1, keepdims=True))
               - jnp.exp(jnp.sum(lp[2:3] * lp[3:4], axis=-1, keepdims=True)) + lam_init)
        a = acc_scr[...]
        on = a[:dv] / a[dv:dv + 1]
        o = on[:, :tq] - lam * on[:, tq:]
        ms = jnp.mean(o * o, axis=0, keepdims=True)
        y = o * lax.rsqrt(ms + EPS) * subg_ref[...] * (1.0 - lam_init)
        o_ref[0] = y.T.astype(o_ref.dtype)


def _diff_attention(z, vt, lam_params, sub_g_col, lam_init, *, tq=256, tk=1024, sk=256):
    B, T, _ = z.shape
    slopes = jnp.exp2(-8.0 * jnp.arange(1, A_HEADS + 1, dtype=F32) / A_HEADS)
    kern = functools.partial(_attn_kernel, tq=tq, tk=tk, sk=sk, lam_init=lam_init)
    hw = 2 * A_QK_DIM
    return pl.pallas_call(
        kern,
        grid=(B, A_HEADS, T // tq, T // tk),
        in_specs=[
            pl.BlockSpec(memory_space=pltpu.SMEM),
            pl.BlockSpec((1, tq, hw), lambda b, h, i, j: (b, i, h)),
            pl.BlockSpec((1, tk, hw), lambda b, h, i, j: (b, j, A_HEADS + h)),
            pl.BlockSpec((1, A_V_DIM, tk), lambda b, h, i, j: (b, h, j)),
            pl.BlockSpec((4, A_QK_DIM), lambda b, h, i, j: (0, 0)),
            pl.BlockSpec((A_V_DIM, 1), lambda b, h, i, j: (0, 0)),
        ],
        out_specs=pl.BlockSpec((1, tq, A_V_DIM), lambda b, h, i, j: (b, i, h)),
        out_shape=jax.ShapeDtypeStruct((B, T, A_HEADS * A_V_DIM), BF16),
        scratch_shapes=[pltpu.VMEM((1, 2 * tq), F32), pltpu.VMEM((A_V_DIM + ONES_ROWS, 2 * tq), F32)],
        compiler_params=_params(("parallel", "parallel", "parallel", "arbitrary")),
        name="diff_attention",
    )(slopes, z, z, vt, lam_params, sub_g_col)


def _gla_kernel(qf_ref, kf_ref, vf_ref, lrf_ref, qb_ref, kb_ref, vb_ref, lrb_ref, gw_ref, gb_ref,
                of_ref, ob_ref, s_scr, *, tg):
    L = B_CHUNK
    nchunk = tg // L
    pw = 2 * B_K_DIM
    vw = 2 * B_V_DIM

    @pl.when(pl.program_id(1) == 0)
    def _():
        s_scr[...] = jnp.zeros(s_scr.shape, F32)

    tt = lax.broadcasted_iota(jnp.int32, (L, L), 0)
    ss = lax.broadcasted_iota(jnp.int32, (L, L), 1)
    lane = lax.broadcasted_iota(jnp.int32, (1, pw), 1)
    srow = lax.broadcasted_iota(jnp.int32, (vw, pw), 0)
    slane = lax.broadcasted_iota(jnp.int32, (vw, pw), 1)
    same_head = (srow >= B_V_DIM) == (slane >= B_K_DIM)
    zero_b = jnp.zeros((), BF16)

    dirs = ((qf_ref, kf_ref, vf_ref, lrf_ref, of_ref), (qb_ref, kb_ref, vb_ref, lrb_ref, ob_ref))
    keeps = (ss <= tt, ss >= tt)
    tris = [kp.astype(BF16) for kp in keeps]
    lgs = []
    for d in range(2):
        pre = _dot_split2(dirs[d][3][0], gw_ref[d]) + gb_ref[d]
        lgs.append(_log_sigmoid(pre) * (1.0 / B_GATE_NORM))
    np_ = B_HEADS // 2
    states = [[s_scr[d, p] for p in range(np_)] for d in range(2)]
    sels = (lane < B_K_DIM, lane >= B_K_DIM)
    csl = [slice(c * L, (c + 1) * L) for c in range(nchunk)]
    ksl = [slice(p * pw, (p + 1) * pw) for p in range(np_)]
    items = [(c, d) for c in range(nchunk) for d in range(2)]
    chains = [(c, d, p) for c, d in items for p in range(np_)]
    bs = {(c, d): _dot_exact_lhs(tris[d], lgs[d][csl[c]]) for c, d in items}
    b_ends = {(c, d): bs[c, d][L - 1:L] if d == 0 else bs[c, d][0:1] for c, d in items}
    qcs = {(c, d): dirs[d][0][0, csl[c], :].astype(F32) for c, d in items}
    kcs = {(c, d): dirs[d][1][0, csl[c], :].astype(F32) for c, d in items}
    qes = {i: (qcs[i] * (B_K_DIM ** -0.5) * jnp.exp(bs[i])).astype(BF16) for i in items}
    kes = {i: (kcs[i] * jnp.exp(-bs[i])).astype(BF16) for i in items}
    kds = {i: (kcs[i] * jnp.exp(b_ends[i] - bs[i])).astype(BF16) for i in items}
    decs = {i: jnp.exp(b_ends[i]) for i in items}
    v_ps = {(c, d, p): dirs[d][2][0, csl[c], p * vw:(p + 1) * vw] for c, d, p in chains}
    atts = {(c, d, p, hh): lax.dot_general(jnp.where(sels[hh], qes[c, d][:, ksl[p]], zero_b), kes[c, d][:, ksl[p]],
                                           _NT, preferred_element_type=F32)
            for c, d, p in chains for hh in range(2)}
    atts = {key: jnp.where(keeps[key[1]], a, 0.0).astype(BF16) for key, a in atts.items()}
    intras = {(c, d, p): jnp.concatenate(
        [jnp.dot(atts[c, d, p, hh], v_ps[c, d, p][:, hh * B_V_DIM:(hh + 1) * B_V_DIM],
                 preferred_element_type=F32) for hh in range(2)], axis=-1) for c, d, p in chains}
    uts = {(c, d, p): jnp.where(same_head, lax.dot_general(v_ps[c, d, p], kds[c, d][:, ksl[p]], _TN,
                                                           preferred_element_type=F32), 0.0)
           for c, d, p in chains}
    for ci in range(nchunk):
        for d in range(2):
            c = ci if d == 0 else nchunk - 1 - ci
            for p in range(np_):
                st = states[d][p]
                inter = lax.dot_general(qes[c, d][:, ksl[p]], st.astype(BF16), _NT, preferred_element_type=F32)
                dirs[d][4][0, csl[c], p * vw:(p + 1) * vw] = inter + intras[c, d, p]
                states[d][p] = decs[c, d][:, ksl[p]] * st + uts[c, d, p]
    for d in range(2):
        for p in range(np_):
            s_scr[d, p] = states[d][p]


def _gla(z, zlr, gw, gb, *, tg=512):
    B, T, _ = z.shape
    nb = T // tg
    kw = B_HEADS * B_K_DIM
    vw = B_HEADS * B_V_DIM
    q_blk = EVEN_BQ // kw
    v_blk = EVEN_BV // vw
    fwd = lambda b, i: (b, i)
    bwd = lambda b, i: (b, nb - 1 - i)

    def specs(order):
        return [
            pl.BlockSpec((1, tg, kw), lambda b, i: (*order(b, i), q_blk)),
            pl.BlockSpec((1, tg, kw), lambda b, i: (*order(b, i), q_blk + 1)),
            pl.BlockSpec((1, tg, vw), lambda b, i: (*order(b, i), v_blk)),
            pl.BlockSpec((1, tg, GATE_PAD), lambda b, i: (*order(b, i), 0)),
        ]

    kern = functools.partial(_gla_kernel, tg=tg)
    return pl.pallas_call(
        kern,
        grid=(B, nb),
        in_specs=specs(fwd) + specs(bwd) + [
            pl.BlockSpec((2, GATE_PAD, kw), lambda b, i: (0, 0, 0)),
            pl.BlockSpec((2, 1, kw), lambda b, i: (0, 0, 0)),
        ],
        out_specs=[
            pl.BlockSpec((1, tg, vw), lambda b, i: (b, i, 0)),
            pl.BlockSpec((1, tg, vw), lambda b, i: (b, nb - 1 - i, 0)),
        ],
        out_shape=[jax.ShapeDtypeStruct((B, T, vw), F32)] * 2,
        scratch_shapes=[pltpu.VMEM((2, B_HEADS // 2, 2 * B_V_DIM, 2 * B_K_DIM), F32)],
        compiler_params=_params(("parallel", "arbitrary")),
        name="gla",
    )(z, z, z, zlr, z, z, z, zlr, gw, gb)


def _head_rmsnorm(o, g, width):
    parts = []
    for h in range(o.shape[-1] // width):
        oh = o[:, h * width:(h + 1) * width]
        ms = jnp.mean(oh * oh, axis=-1, keepdims=True)
        parts.append(oh * lax.rsqrt(ms + EPS) * g[:, h * width:(h + 1) * width])
    return jnp.concatenate(parts, axis=-1)


def _outproj_even_kernel(a_ref, of_ref, ob_ref, bg_ref, x_ref, g1_ref, gn_ref, w_ref, o_ref):
    na = A_HEADS * A_V_DIM
    o = _head_rmsnorm(of_ref[0] + ob_ref[0], gn_ref[...], B_V_DIM)
    b_out = (o * _silu(bg_ref[0].astype(F32))).astype(BF16)
    res = jnp.dot(a_ref[0], w_ref[0:na, :], preferred_element_type=F32)
    res = res + jnp.dot(b_out, w_ref[na:, :], preferred_element_type=F32)
    o_ref[0] = x_ref[0] + g1_ref[0] * res


def _outproj_even(a, of, ob, z, x, g1, gn, w, *, tm=512):
    B, T, _ = x.shape
    nm = a.shape[-1]
    bg_blk = EVEN_BG // nm
    tok = lambda b, i: (b, i, 0)
    return pl.pallas_call(
        _outproj_even_kernel,
        grid=(B, T // tm),
        in_specs=[
            pl.BlockSpec((1, tm, nm), tok),
            pl.BlockSpec((1, tm, nm), tok),
            pl.BlockSpec((1, tm, nm), tok),
            pl.BlockSpec((1, tm, nm), lambda b, i: (b, i, bg_blk)),
            pl.BlockSpec((1, tm, D_MODEL), tok),
            pl.BlockSpec((1, 1, D_MODEL), lambda b, i: (b, 0, 0)),
            pl.BlockSpec((1, nm), lambda b, i: (0, 0)),
            pl.BlockSpec((2 * nm, D_MODEL), lambda b, i: (0, 0)),
        ],
        out_specs=pl.BlockSpec((1, tm, D_MODEL), tok),
        out_shape=jax.ShapeDtypeStruct((B, T, D_MODEL), F32),
        compiler_params=_params(("parallel", "parallel")),
        name="outproj_even",
    )(a, of, ob, z, x, g1, gn, w)


def _mlstm_kernel(gbr_ref, gbc_ref, qf_ref, kf_ref, vf_ref, gcf_ref, grf_ref, qb_ref, kb_ref, vb_ref, gcb_ref,
                  grb_ref, of_ref, ob_ref, c_scr, n_scr, m_scr, *, tg):
    L = C_CHUNK
    nchunk = tg // L
    dk, dv = C_QK_DIM, C_V_DIM

    @pl.when(pl.program_id(1) == 0)
    def _():
        c_scr[...] = jnp.zeros(c_scr.shape, F32)
        n_scr[...] = jnp.zeros(n_scr.shape, F32)
        m_scr[...] = jnp.zeros(m_scr.shape, F32)

    tt = lax.broadcasted_iota(jnp.int32, (L, L), 0)
    ss = lax.broadcasted_iota(jnp.int32, (L, L), 1)

    dirs = ((qf_ref, kf_ref, vf_ref, gcf_ref, grf_ref, of_ref),
            (qb_ref, kb_ref, vb_ref, gcb_ref, grb_ref, ob_ref))
    keeps = (ss <= tt, ss >= tt)
    tri_col = [keeps[0].astype(BF16), keeps[1].astype(BF16)]
    tri_row = [keeps[1].astype(BF16), keeps[0].astype(BF16)]
    glane = lax.broadcasted_iota(jnp.int32, (1, GATE_PAD), 1)
    f_lane = jnp.logical_and((glane & C_HEADS) != 0, glane < 4 * C_HEADS)
    f_row = (lax.broadcasted_iota(jnp.int32, (4 * C_HEADS, 1), 0) & C_HEADS) != 0
    csl = [slice(c * L, (c + 1) * L) for c in range(nchunk)]
    items = [(c, d) for c in range(nchunk) for d in range(2)]
    chains = [(c, d, h) for c, d in items for h in range(C_HEADS)]
    ii = {(d, h): (2 * d) * C_HEADS + h for d in range(2) for h in range(C_HEADS)}
    fi = {(d, h): (2 * d + 1) * C_HEADS + h for d in range(2) for h in range(C_HEADS)}

    Gc = []
    for d in range(2):
        g = dirs[d][3][0] + gbr_ref[...]
        Gc.append(jnp.where(f_lane, _log_sigmoid(g), g))
    Gr = {}
    for c, d in items:
        g = dirs[d][4][0, c] + gbc_ref[...]
        Gr[c, d] = jnp.where(f_row, _log_sigmoid(g), g)
    Fc = {(c, d): _dot_exact_lhs(tri_col[d], Gc[d][csl[c]]) for c, d in items}
    Fr = {(c, d): _dot_exact_rhs(Gr[c, d], tri_row[d]) for c, d in items}
    F_row = {(c, d, h): Fr[c, d][fi[d, h]:fi[d, h] + 1, :] for c, d, h in chains}
    i_row = {(c, d, h): Gr[c, d][ii[d, h]:ii[d, h] + 1, :] for c, d, h in chains}
    F_col = {(c, d, h): Fc[c, d][:, fi[d, h]:fi[d, h] + 1] for c, d, h in chains}
    i_col = {(c, d, h): Gc[d][csl[c], ii[d, h]:ii[d, h] + 1] for c, d, h in chains}
    FL = {(c, d, h): F_row[c, d, h][:, L - 1:L] if d == 0 else F_row[c, d, h][:, 0:1] for c, d, h in chains}
    wl_max = {k_: jnp.max(FL[k_] - F_row[k_] + i_row[k_], axis=1, keepdims=True) for k_ in chains}
    m_old, m_new = {}, {}
    m_cur = {(d, h): m_scr[d, h] for d in range(2) for h in range(C_HEADS)}
    for ci in range(nchunk):
        for d in range(2):
            c = ci if d == 0 else nchunk - 1 - ci
            for h in range(C_HEADS):
                m_old[c, d, h] = m_cur[d, h]
                m_new[c, d, h] = jnp.maximum(FL[c, d, h] + m_cur[d, h], wl_max[c, d, h])
                m_cur[d, h] = m_new[c, d, h]
    Dm = {k_: jnp.where(keeps[k_[1]], F_col[k_] - F_row[k_] + i_row[k_], -jnp.inf) for k_ in chains}
    a_col = {k_: F_col[k_] + m_old[k_] for k_ in chains}
    mt = {k_: jnp.maximum(a_col[k_], jnp.max(Dm[k_], axis=1, keepdims=True)) for k_ in chains}
    qc = {(c, d, h): dirs[d][0][0, csl[c], h * dk:(h + 1) * dk] for c, d, h in chains}
    kf = {(c, d, h): dirs[d][1][0, csl[c], h * dk:(h + 1) * dk].astype(F32) * (dk ** -0.5) for c, d, h in chains}
    vc = {(c, d, h): dirs[d][2][0, csl[c], h * dv:(h + 1) * dv] for c, d, h in chains}
    qk = {k_: lax.dot_general(qc[k_], kf[k_].astype(BF16), _NT, preferred_element_type=F32) for k_ in chains}
    W = {k_: jnp.exp(Dm[k_] - mt[k_]) * qk[k_] for k_ in chains}
    inter = {k_: jnp.exp(a_col[k_] - mt[k_]) for k_ in chains}
    w_sum = {k_: jnp.sum(W[k_], axis=1, keepdims=True) for k_ in chains}
    e_mt = {k_: jnp.exp(-mt[k_]) for k_ in chains}
    Wv = {k_: jnp.dot(W[k_].astype(BF16), vc[k_], preferred_element_type=F32) for k_ in chains}
    kw = {k_: kf[k_] * jnp.exp(FL[k_] - F_col[k_] + i_col[k_] - m_new[k_]) for k_ in chains}
    kv = {k_: lax.dot_general(kw[k_].astype(BF16), vc[k_], _TN, preferred_element_type=F32) for k_ in chains}
    k_sum = {k_: jnp.sum(kw[k_], axis=0, keepdims=True) for k_ in chains}
    dec = {k_: jnp.exp(FL[k_] + m_old[k_] - m_new[k_]) for k_ in chains}
    Cs = {(d, h): c_scr[d, h] for d in range(2) for h in range(C_HEADS)}
    ns = {(d, h): n_scr[d, h] for d in range(2) for h in range(C_HEADS)}
    for ci in range(nchunk):
        for d in range(2):
            c = ci if d == 0 else nchunk - 1 - ci
            for h in range(C_HEADS):
                k_ = (c, d, h)
                C, n = Cs[d, h], ns[d, h]
                num = inter[k_] * jnp.dot(qc[k_], C.astype(BF16), preferred_element_type=F32) + Wv[k_]
                qn = jnp.sum(qc[k_].astype(F32) * n, axis=1, keepdims=True)
                den = inter[k_] * qn + w_sum[k_]
                dirs[d][5][0, csl[c], h * dv:(h + 1) * dv] = num / jnp.maximum(jnp.abs(den), e_mt[k_])
                Cs[d, h] = dec[k_] * C + kv[k_]
                ns[d, h] = dec[k_] * n + k_sum[k_]
    for d in range(2):
        for h in range(C_HEADS):
            c_scr[d, h] = Cs[d, h]
            n_scr[d, h] = ns[d, h]
            m_scr[d, h] = m_cur[d, h]


def _mlstm(z, gcol, grow, gate_b, *, tg=256):
    B, T, _ = z.shape
    nb = T // tg
    qw = C_HEADS * C_QK_DIM
    vw = C_HEADS * C_V_DIM
    ng = grow.shape[2]
    fwd = lambda b, i: (b, i)
    bwd = lambda b, i: (b, nb - 1 - i)

    def specs(order):
        return [
            pl.BlockSpec((1, tg, qw), lambda b, i: (*order(b, i), 0)),
            pl.BlockSpec((1, tg, qw), lambda b, i: (*order(b, i), 1)),
            pl.BlockSpec((1, tg, vw), lambda b, i: (*order(b, i), 1)),
            pl.BlockSpec((1, tg, GATE_PAD), lambda b, i: (*order(b, i), 0)),
            pl.BlockSpec((1, tg // C_CHUNK, ng, C_CHUNK), lambda b, i: (*order(b, i), 0, 0)),
        ]

    kern = functools.partial(_mlstm_kernel, tg=tg)
    return pl.pallas_call(
        kern,
        grid=(B, nb),
        in_specs=[pl.BlockSpec((1, GATE_PAD), lambda b, i: (0, 0)),
                  pl.BlockSpec((ng, 1), lambda b, i: (0, 0))] + specs(fwd) + specs(bwd),
        out_specs=[
            pl.BlockSpec((1, tg, vw), lambda b, i: (b, i, 0)),
            pl.BlockSpec((1, tg, vw), lambda b, i: (b, nb - 1 - i, 0)),
        ],
        out_shape=[jax.ShapeDtypeStruct((B, T, vw), F32)] * 2,
        scratch_shapes=[
            pltpu.VMEM((2, C_HEADS, C_QK_DIM, C_V_DIM), F32),
            pltpu.VMEM((2, C_HEADS, 1, C_QK_DIM), F32),
            pltpu.VMEM((2, C_HEADS, 1, 1), F32),
        ],
        compiler_params=_params(("parallel", "arbitrary")),
        name="mlstm",
    )(_pad_cols(gate_b[None, :], GATE_PAD), gate_b[:, None], z, z, z, gcol, grow, z, z, z, gcol, grow)


def _outproj_odd_kernel(hf_ref, hb_ref, co_ref, x_ref, g1_ref, gn_ref, w_ref, o_ref):
    ht = _head_rmsnorm(hf_ref[0] + hb_ref[0], gn_ref[...], C_V_DIM)
    mix = (jax.nn.sigmoid(co_ref[0].astype(F32)) * ht).astype(BF16)
    o_ref[0] = x_ref[0] + g1_ref[0] * jnp.dot(mix, w_ref[...], preferred_element_type=F32)


def _outproj_odd(hf, hb, z, x, g1, gn, w, *, tm=512):
    B, T, _ = x.shape
    nm = hf.shape[-1]
    tok = lambda b, i: (b, i, 0)
    return pl.pallas_call(
        _outproj_odd_kernel,
        grid=(B, T // tm),
        in_specs=[
            pl.BlockSpec((1, tm, nm), tok),
            pl.BlockSpec((1, tm, nm), tok),
            pl.BlockSpec((1, tm, nm), lambda b, i: (b, i, 2)),
            pl.BlockSpec((1, tm, D_MODEL), tok),
            pl.BlockSpec((1, 1, D_MODEL), lambda b, i: (b, 0, 0)),
            pl.BlockSpec((1, nm), lambda b, i: (0, 0)),
            pl.BlockSpec((nm, D_MODEL), lambda b, i: (0, 0)),
        ],
        out_specs=pl.BlockSpec((1, tm, D_MODEL), tok),
        out_shape=jax.ShapeDtypeStruct((B, T, D_MODEL), F32),
        compiler_params=_params(("parallel", "parallel")),
        name="outproj_odd",
    )(hf, hb, z, x, g1, gn, w)


def _ffn_kernel(x_ref, sh_ref, sc_ref, g_ref, g2_ref, wg_ref, wu_ref, wd_ref, fg_ref, o_ref, h_scr, acc_scr,
                *, final_norm):
    j = pl.program_id(2)

    @pl.when(j == 0)
    def _():
        h_scr[...] = _norm_mod(x_ref[0], g_ref[...], sh_ref[0], sc_ref[0]).astype(BF16)
        acc_scr[...] = jnp.zeros(acc_scr.shape, F32)

    hb = h_scr[...]
    gate = jnp.dot(hb, wg_ref[...], preferred_element_type=F32)
    up = jnp.dot(hb, wu_ref[...], preferred_element_type=F32)
    act = (_silu(gate) * up).astype(BF16)
    acc_scr[...] += jnp.dot(act, wd_ref[...], preferred_element_type=F32)

    @pl.when(j == pl.num_programs(2) - 1)
    def _():
        y = x_ref[0] + g2_ref[0] * acc_scr[...]
        if final_norm:
            ms = jnp.mean(y * y, axis=-1, keepdims=True)
            y = y * lax.rsqrt(ms + EPS) * fg_ref[...]
        o_ref[0] = y


def _ffn(x, sh, sc, g, g2, w_gu, w_down, final_g, *, final_norm, tm=512, tf=1408):
    B, T, _ = x.shape
    nf = D_FF // tf
    tok = lambda b, i, j: (b, i, 0)
    per_b = lambda b, i, j: (b, 0, 0)
    const = lambda b, i, j: (0, 0)
    kern = functools.partial(_ffn_kernel, final_norm=final_norm)
    return pl.pallas_call(
        kern,
        grid=(B, T // tm, nf),
        in_specs=[
            pl.BlockSpec((1, tm, D_MODEL), tok),
            pl.BlockSpec((1, 1, D_MODEL), per_b),
            pl.BlockSpec((1, 1, D_MODEL), per_b),
            pl.BlockSpec((1, D_MODEL), const),
            pl.BlockSpec((1, 1, D_MODEL), per_b),
            pl.BlockSpec((D_MODEL, tf), lambda b, i, j: (0, j)),
            pl.BlockSpec((D_MODEL, tf), lambda b, i, j: (0, nf + j)),
            pl.BlockSpec((tf, D_MODEL), lambda b, i, j: (j, 0)),
            pl.BlockSpec((1, D_MODEL), const),
        ],
        out_specs=pl.BlockSpec((1, tm, D_MODEL), tok),
        out_shape=jax.ShapeDtypeStruct((B, T, D_MODEL), F32),
        scratch_shapes=[pltpu.VMEM((tm, D_MODEL), BF16), pltpu.VMEM((tm, D_MODEL), F32)],
        compiler_params=_params(("parallel", "parallel", "arbitrary")),
        name="ffn_final" if final_norm else "ffn",
    )(x, sh, sc, g, g2, w_gu, w_gu, w_down, final_g)


def _pad_cols(w, n):
    return jnp.pad(w, ((0, 0), (0, n - w.shape[1])))


def _trunk(x, mod, p):
    for i in range(DEPTH):
        sh1, sc1, g1, sh2, sc2, g2 = [m[:, None, :] for m in jnp.split(mod[i], 6, axis=-1)]
        if i % 2 == 0:
            j = i // 2
            lam_init = 0.8 - 0.6 * math.exp(-0.3 * i)
            z, zlr, vt = _inproj(x, sh1, sc1, p["norm1_g"][i], p["even_w_main"][j], p["even_w_lr"][j],
                                 p["even_w_vt"][j], with_rows=False)
            a = _diff_attention(z, vt, p["even_lam"][j], p["even_sub_g"][j], lam_init)
            of, ob = _gla(z, zlr, p["even_gk_w"][j], p["even_gk_b"][j])
            x = _outproj_even(a, of, ob, z, x, g1, p["even_gla_norm_g"][j], p["even_w_out"][j])
        else:
            j = i // 2
            z, gcol, grow = _inproj(x, sh1, sc1, p["norm1_g"][i], p["odd_w_main"][j], p["odd_w_gate"][j],
                                    p["odd_w_gate_t"][j], with_rows=True, chunk=C_CHUNK)
            hf, hb = _mlstm(z, gcol, grow, p["odd_gate_b"][j])
            x = _outproj_odd(hf, hb, z, x, g1, p["odd_norm_g"][j], p["odd_w_out"][j])
        x = _ffn(x, sh2, sc2, p["norm2_g"][i], g2, p["ffn_w_gu"][i], p["ffn_w_down"][i], p["final_g"],
                 final_norm=(i == DEPTH - 1))
    return x


def kernel(x_prompt, x_sample, c_prompt, c_sample, w_mod, b_mod, norm1_g, norm2_g, even_w_in, even_lam_q1, even_lam_k1, even_lam_q2, even_lam_k2, even_attn_sub_g, even_gk_w_f, even_gk_b_f, even_gk_w_b, even_gk_b_b, even_gla_norm_g, even_w_out, odd_w_in, odd_gate_b, odd_norm_g, odd_w_out, ffn_w_gu, ffn_w_down, final_g):
    n_even = even_w_in.shape[0]
    n_odd = odd_w_in.shape[0]
    r = B_GATE_RANK
    kw = B_HEADS * B_K_DIM
    gk_w = jnp.zeros((n_even, 2, GATE_PAD, kw), F32)
    gk_w = gk_w.at[:, 0, 0:r].set(even_gk_w_f).at[:, 1, r:2 * r].set(even_gk_w_b)
    p = {
        "norm1_g": norm1_g[:, None, :],
        "norm2_g": norm2_g[:, None, :],
        "final_g": final_g[None, :],
        "even_w_main": jnp.concatenate([even_w_in[:, :, :EVEN_AV], even_w_in[:, :, EVEN_REST:EVEN_MAIN]],
                                       axis=-1).astype(BF16),
        "even_w_vt": jnp.swapaxes(even_w_in[:, :, EVEN_AV:EVEN_REST], 1, 2).astype(BF16),
        "even_w_lr": jnp.stack([_pad_cols(even_w_in[j, :, EVEN_MAIN:], GATE_PAD) for j in range(n_even)]),
        "even_lam": jnp.stack([even_lam_q1, even_lam_k1, even_lam_q2, even_lam_k2], axis=1),
        "even_sub_g": even_attn_sub_g[:, :, None],
        "even_gk_w": gk_w,
        "even_gk_b": jnp.stack([even_gk_b_f, even_gk_b_b], axis=1)[:, :, None, :],
        "even_gla_norm_g": even_gla_norm_g[:, None, :],
        "even_w_out": even_w_out.astype(BF16),
        "odd_w_main": odd_w_in[:, :, :ODD_MAIN].astype(BF16),
        "odd_w_gate": jnp.stack([_pad_cols(odd_w_in[j, :, ODD_MAIN:], GATE_PAD) for j in range(n_odd)]),
        "odd_w_gate_t": jnp.swapaxes(odd_w_in[:, :, ODD_MAIN:], 1, 2),
        "odd_gate_b": odd_gate_b,
        "odd_norm_g": odd_norm_g[:, None, :],
        "odd_w_out": odd_w_out.astype(BF16),
        "ffn_w_gu": ffn_w_gu.astype(BF16),
        "ffn_w_down": ffn_w_down.astype(BF16),
    }
    nbp = c_prompt.shape[0]
    mod = _modulation(jnp.concatenate([c_prompt, c_sample], axis=0), w_mod, b_mod)
    y_prompt = _trunk(x_prompt, mod[:, :nbp], p)
    y_sample = _trunk(x_sample, mod[:, nbp:], p)
    return (y_prompt, y_sample)
```

```python
import functools
import math

import jax
import jax.numpy as jnp
from jax import lax
from jax.experimental import pallas as pl
from jax.experimental.pallas import tpu as pltpu

F32 = jnp.float32
BF16 = jnp.bfloat16
HIGHEST = lax.Precision.HIGHEST

D_MODEL = 1024
DEPTH = 2
EPS = 1e-6
A_HEADS = 4
A_QK_DIM = 64
A_V_DIM = 128
B_HEADS = 4
B_K_DIM = 64
B_V_DIM = 128
B_GATE_RANK = 16
B_GATE_NORM = 16.0
B_CHUNK = 64
C_HEADS = 4
C_QK_DIM = 128
C_V_DIM = 256
C_CHUNK = 64
D_FF = 2816
EVEN_AV = 1024
EVEN_REST = 1536
EVEN_MAIN = 3072
EVEN_BQ = 1024
EVEN_BV = 1536
EVEN_BG = 2048
ODD_MAIN = 3072
ODD_CO = 2048
GATE_PAD = 128
ONES_ROWS = 16
LOG2E = 1.4426950408889634
ATTN_SUB = 256
ATTN_AHEAD = 2

VMEM_LIMIT = 56 * 1024 * 1024

_NT = (((1,), (1,)), ((), ()))
_TN = (((0,), (0,)), ((), ()))


def _log_sigmoid(x):
    return jnp.minimum(x, 0.0) - jnp.log1p(jnp.exp(-jnp.abs(x)))


def _silu(x):
    return x * jax.nn.sigmoid(x)


def _split_bf16(x, terms):
    parts = []
    for _ in range(terms - 1):
        hi = x.astype(BF16)
        parts.append(hi)
        x = x - hi.astype(F32)
    parts.append(x.astype(BF16))
    return parts


def _dot_exact_lhs(a_bf16, x):
    return sum(jnp.dot(a_bf16, part, preferred_element_type=F32) for part in _split_bf16(x, 3))


def _dot_exact_rhs(x, a_bf16):
    return sum(jnp.dot(part, a_bf16, preferred_element_type=F32) for part in _split_bf16(x, 3))


def _dot_split2(x, w, dims=(((1,), (0,)), ((), ()))):
    xh, xl = _split_bf16(x, 2)
    wh, wl = _split_bf16(w, 2)
    dot = functools.partial(lax.dot_general, dimension_numbers=dims, preferred_element_type=F32)
    return dot(xh, wh) + dot(xh, wl) + dot(xl, wh)


def _params(sem):
    return pltpu.CompilerParams(dimension_semantics=sem, vmem_limit_bytes=VMEM_LIMIT)


def _mod_kernel(c_ref, w_ref, b_ref, o_ref):
    s = _silu(c_ref[...])
    o_ref[0] = jnp.dot(s, w_ref[0], precision=HIGHEST, preferred_element_type=F32) + b_ref[0]


def _modulation(c_all, w_mod, b_mod):
    nb = c_all.shape[0]
    tn = 1536
    return pl.pallas_call(
        _mod_kernel,
        grid=(DEPTH, 6 * D_MODEL // tn),
        in_specs=[
            pl.BlockSpec((nb, D_MODEL), lambda i, j: (0, 0)),
            pl.BlockSpec((1, D_MODEL, tn), lambda i, j: (i, 0, j)),
            pl.BlockSpec((1, 1, tn), lambda i, j: (i, 0, j)),
        ],
        out_specs=pl.BlockSpec((1, nb, tn), lambda i, j: (i, 0, j)),
        out_shape=jax.ShapeDtypeStruct((DEPTH, nb, 6 * D_MODEL), F32),
        compiler_params=_params(("parallel", "parallel")),
        name="modulation",
    )(c_all, w_mod, b_mod.reshape(DEPTH, 1, 6 * D_MODEL))


def _norm_mod(x, g, sh, sc):
    ms = jnp.mean(x * x, axis=-1, keepdims=True)
    y = x * lax.rsqrt(ms + EPS) * g
    return y * (1.0 + sc) + sh


def _inproj_kernel(x_ref, sh_ref, sc_ref, g_ref, w_ref, wg_ref, wt_ref, z_ref, zg_ref, zt_ref,
                   *, n_main, n_col, chunk, with_rows):
    h = _norm_mod(x_ref[0], g_ref[...], sh_ref[0], sc_ref[0])
    hb = h.astype(BF16)
    for j in range(n_main // n_col):
        sl = slice(j * n_col, (j + 1) * n_col)
        z_ref[0, :, sl] = jnp.dot(hb, w_ref[:, sl], preferred_element_type=F32).astype(BF16)
    zg_ref[0] = _dot_split2(h, wg_ref[...])
    if with_rows:
        tm = h.shape[0]
        for c in range(tm // chunk):
            zt_ref[0, c] = _dot_split2(wt_ref[...], h[c * chunk:(c + 1) * chunk], _NT)
    else:
        for c in range(h.shape[0] // ATTN_SUB):
            zt_ref[0, c] = lax.dot_general(wt_ref[...], hb[c * ATTN_SUB:(c + 1) * ATTN_SUB], _NT,
                                           preferred_element_type=F32).astype(BF16)


def _inproj(x, sh, sc, g, w_main, w_gate, w_t, *, with_rows, tm=512, chunk=64):
    B, T, _ = x.shape
    n_main = w_main.shape[1]
    kern = functools.partial(_inproj_kernel, n_main=n_main, n_col=512, chunk=chunk, with_rows=with_rows)
    in_specs = [
        pl.BlockSpec((1, tm, D_MODEL), lambda b, i: (b, i, 0)),
        pl.BlockSpec((1, 1, D_MODEL), lambda b, i: (b, 0, 0)),
        pl.BlockSpec((1, 1, D_MODEL), lambda b, i: (b, 0, 0)),
        pl.BlockSpec((1, D_MODEL), lambda b, i: (0, 0)),
        pl.BlockSpec((D_MODEL, n_main), lambda b, i: (0, 0)),
        pl.BlockSpec((D_MODEL, GATE_PAD), lambda b, i: (0, 0)),
    ]
    out_specs = [
        pl.BlockSpec((1, tm, n_main), lambda b, i: (b, i, 0)),
        pl.BlockSpec((1, tm, GATE_PAD), lambda b, i: (b, i, 0)),
    ]
    out_shape = [
        jax.ShapeDtypeStruct((B, T, n_main), BF16),
        jax.ShapeDtypeStruct((B, T, GATE_PAD), F32),
    ]
    args = [x, sh, sc, g, w_main, w_gate, w_t]
    nt = w_t.shape[0]
    in_specs.append(pl.BlockSpec((nt, D_MODEL), lambda b, i: (0, 0)))
    if with_rows:
        out_specs.append(pl.BlockSpec((1, tm // chunk, nt, chunk), lambda b, i: (b, i, 0, 0)))
        out_shape.append(jax.ShapeDtypeStruct((B, T // chunk, nt, chunk), F32))
    else:
        out_specs.append(pl.BlockSpec((1, tm // ATTN_SUB, nt, ATTN_SUB), lambda b, i: (b, i, 0, 0)))
        out_shape.append(jax.ShapeDtypeStruct((B, T // ATTN_SUB, nt, ATTN_SUB), BF16))
    return pl.pallas_call(
        kern,
        grid=(B, T // tm),
        in_specs=in_specs,
        out_specs=out_specs,
        out_shape=out_shape,
        compiler_params=_params(("parallel", "parallel")),
        name="inproj_rows" if with_rows else "inproj",
    )(*args)


def _attn_kernel(slope_ref, q_ref, k_ref, vt_ref, lamp_ref, subg_ref, o_ref, *, n_sub, lam_init):
    h = pl.program_id(1)
    qi = pl.program_id(2)
    tq = sk = ATTN_SUB
    dv = A_V_DIM
    hw = 2 * A_QK_DIM
    slope2 = slope_ref[h, 0] + slope_ref[h, 1] + slope_ref[h, 2]

    lane = lax.broadcasted_iota(jnp.int32, (1, hw), 1)
    qs = (q_ref[0].astype(F32) * (A_QK_DIM ** -0.5 * LOG2E)).astype(BF16)
    zero = jnp.zeros((), BF16)
    c_off = lax.broadcasted_iota(jnp.int32, (tq, hw), 0).astype(F32)
    r_off = lax.broadcasted_iota(jnp.int32, (sk, hw), 0).astype(F32)
    q_feat = jnp.zeros((tq, hw), F32)
    k_feat = jnp.zeros((sk, hw), F32)
    for x in range(3):
        q_feat = jnp.where(lane == x, slope_ref[h, x], q_feat)
        q_feat = jnp.where(lane == 3 + x, -c_off, q_feat)
        k_feat = jnp.where(lane == x, r_off, k_feat)
        k_feat = jnp.where(lane == 3 + x, slope_ref[h, x], k_feat)
    q_feat = q_feat.astype(BF16)
    k_feat_pos = k_feat.astype(BF16)
    k_feat_neg = (-k_feat).astype(BF16)
    q_both = jnp.concatenate(
        [jnp.concatenate([jnp.where(lane < A_QK_DIM, qs, zero), q_feat], axis=1),
         jnp.concatenate([jnp.where(lane >= A_QK_DIM, qs, zero), q_feat], axis=1)], axis=0)
    rel = (lax.broadcasted_iota(jnp.int32, (sk, tq), 0)
           - lax.broadcasted_iota(jnp.int32, (sk, tq), 1)).astype(F32)
    diag_bias = -slope2 * jnp.abs(rel)
    ones = jnp.ones((ONES_ROWS, sk), BF16)

    def scores(j):
        if j == 0:
            idx = qi
            const = jnp.zeros((), F32)
            feat = jnp.zeros((sk, hw), BF16)
        else:
            idx = qi + j
            idx = jnp.where(idx >= n_sub, idx - n_sub, idx)
            before = idx < qi
            const = jnp.where(before, 1.0, -1.0) * slope2 * ((idx - qi) * sk).astype(F32)
            feat = jnp.where(before, k_feat_pos, k_feat_neg)
        k_aug = jnp.concatenate([k_ref[0, pl.ds(pl.multiple_of(idx * sk, sk), sk), :], feat], axis=1)
        s = lax.dot_general(k_aug, q_both, _NT, preferred_element_type=F32)
        if j == 0:
            s = s + jnp.concatenate([diag_bias, diag_bias], axis=1)
        return s, jnp.max(s, axis=0, keepdims=True) + const, const, idx

    m = jnp.full((1, 2 * tq), -jnp.inf, F32)
    acc = jnp.zeros((dv + ONES_ROWS, 2 * tq), F32)
    def accumulate(acc, alpha, p, idx):
        vt1 = jnp.concatenate([vt_ref[0, idx], ones], axis=0)
        return alpha * acc + jnp.dot(vt1, p, preferred_element_type=F32)

    ahead = [scores(j) for j in range(min(ATTN_AHEAD, n_sub))]
    pending = None
    for j in range(n_sub):
        s, m_loc, const, idx = ahead.pop(0)
        if j + ATTN_AHEAD < n_sub:
            ahead.append(scores(j + ATTN_AHEAD))
        m_new = jnp.maximum(m, m_loc)
        alpha = jnp.exp2(m - m_new)
        p = jnp.exp2(s - (m_new - const)).astype(BF16)
        m = m_new
        if pending is not None:
            acc = accumulate(acc, *pending)
        pending = (alpha, p, idx)
    acc = accumulate(acc, *pending)

    lp = lamp_ref[...]
    lam = (jnp.exp(jnp.sum(lp[0:1] * lp[1:2], axis=-1, keepdims=True))
           - jnp.exp(jnp.sum(lp[2:3] * lp[3:4], axis=-1, keepdims=True)) + lam_init)
    on = acc[:dv] / acc[dv:dv + 1]
    o = on[:, :tq] - lam * on[:, tq:]
    ms = jnp.mean(o * o, axis=0, keepdims=True)
    y = o * lax.rsqrt(ms + EPS) * subg_ref[...] * (1.0 - lam_init)
    o_ref[0] = y.T.astype(o_ref.dtype)


def _diff_attention(z, vt, lam_params, sub_g_col, lam_init):
    B, T, _ = z.shape
    n_sub = T // ATTN_SUB
    slope2 = jnp.exp2(-8.0 * jnp.arange(1, A_HEADS + 1, dtype=F32) / A_HEADS) * LOG2E
    s0 = slope2.astype(BF16).astype(F32)
    s1 = (slope2 - s0).astype(BF16).astype(F32)
    s2 = (slope2 - s0 - s1).astype(BF16).astype(F32)
    slopes = jnp.stack([s0, s1, s2], axis=1)
    kern = functools.partial(_attn_kernel, n_sub=n_sub, lam_init=lam_init)
    hw = 2 * A_QK_DIM
    return pl.pallas_call(
        kern,
        grid=(B, A_HEADS, n_sub),
        in_specs=[
            pl.BlockSpec(memory_space=pltpu.SMEM),
            pl.BlockSpec((1, ATTN_SUB, hw), lambda b, h, i: (b, i, h)),
            pl.BlockSpec((1, T, hw), lambda b, h, i: (b, 0, A_HEADS + h)),
            pl.BlockSpec((1, n_sub, A_V_DIM, ATTN_SUB), lambda b, h, i: (b, 0, h, 0)),
            pl.BlockSpec((4, A_QK_DIM), lambda b, h, i: (0, 0)),
            pl.BlockSpec((A_V_DIM, 1), lambda b, h, i: (0, 0)),
        ],
        out_specs=pl.BlockSpec((1, ATTN_SUB, A_V_DIM), lambda b, h, i: (b, i, h)),
        out_shape=jax.ShapeDtypeStruct((B, T, A_HEADS * A_V_DIM), BF16),
        compiler_params=_params(("parallel", "parallel", "parallel")),
        name="diff_attention",
    )(slopes, z, z, vt, lam_params, sub_g_col)


def _gla_kernel(qf_ref, kf_ref, vf_ref, lrf_ref, qb_ref, kb_ref, vb_ref, lrb_ref, gw_ref, gb_ref,
                of_ref, ob_ref, s_scr, *, tg):
    L = B_CHUNK
    nchunk = tg // L
    pw = 2 * B_K_DIM
    vw = 2 * B_V_DIM

    @pl.when(pl.program_id(1) == 0)
    def _():
        s_scr[...] = jnp.zeros(s_scr.shape, F32)

    tt = lax.broadcasted_iota(jnp.int32, (L, L), 0)
    ss = lax.broadcasted_iota(jnp.int32, (L, L), 1)
    lane = lax.broadcasted_iota(jnp.int32, (1, pw), 1)
    srow = lax.broadcasted_iota(jnp.int32, (vw, pw), 0)
    slane = lax.broadcasted_iota(jnp.int32, (vw, pw), 1)
    same_head = (srow >= B_V_DIM) == (slane >= B_K_DIM)
    zero_b = jnp.zeros((), BF16)

    dirs = ((qf_ref, kf_ref, vf_ref, lrf_ref, of_ref), (qb_ref, kb_ref, vb_ref, lrb_ref, ob_ref))
    keeps = (ss <= tt, ss >= tt)
    tris = [kp.astype(BF16) for kp in keeps]
    lgs = []
    for d in range(2):
        pre = _dot_split2(dirs[d][3][0], gw_ref[d]) + gb_ref[d]
        lgs.append(_log_sigmoid(pre) * (1.0 / B_GATE_NORM))
    np_ = B_HEADS // 2
    states = [[s_scr[d, p] for p in range(np_)] for d in range(2)]
    sels = (lane < B_K_DIM, lane >= B_K_DIM)
    csl = [slice(c * L, (c + 1) * L) for c in range(nchunk)]
    ksl = [slice(p * pw, (p + 1) * pw) for p in range(np_)]
    items = [(c, d) for c in range(nchunk) for d in range(2)]
    chains = [(c, d, p) for c, d in items for p in range(np_)]
    bs = {(c, d): _dot_exact_lhs(tris[d], lgs[d][csl[c]]) for c, d in items}
    b_ends = {(c, d): bs[c, d][L - 1:L] if d == 0 else bs[c, d][0:1] for c, d in items}
    qcs = {(c, d): dirs[d][0][0, csl[c], :].astype(F32) for c, d in items}
    kcs = {(c, d): dirs[d][1][0, csl[c], :].astype(F32) for c, d in items}
    qes = {i: (qcs[i] * (B_K_DIM ** -0.5) * jnp.exp(bs[i])).astype(BF16) for i in items}
    kes = {i: (kcs[i] * jnp.exp(-bs[i])).astype(BF16) for i in items}
    kds = {i: (kcs[i] * jnp.exp(b_ends[i] - bs[i])).astype(BF16) for i in items}
    decs = {i: jnp.exp(b_ends[i]) for i in items}
    v_ps = {(c, d, p): dirs[d][2][0, csl[c], p * vw:(p + 1) * vw] for c, d, p in chains}
    atts = {(c, d, p, hh): lax.dot_general(jnp.where(sels[hh], qes[c, d][:, ksl[p]], zero_b), kes[c, d][:, ksl[p]],
                                           _NT, preferred_element_type=F32)
            for c, d, p in chains for hh in range(2)}
    atts = {key: jnp.where(keeps[key[1]], a, 0.0).astype(BF16) for key, a in atts.items()}
    intras = {(c, d, p): jnp.concatenate(
        [jnp.dot(atts[c, d, p, hh], v_ps[c, d, p][:, hh * B_V_DIM:(hh + 1) * B_V_DIM],
                 preferred_element_type=F32) for hh in range(2)], axis=-1) for c, d, p in chains}
    uts = {(c, d, p): jnp.where(same_head, lax.dot_general(v_ps[c, d, p], kds[c, d][:, ksl[p]], _TN,
                                                           preferred_element_type=F32), 0.0)
           for c, d, p in chains}
    for ci in range(nchunk):
        for d in range(2):
            c = ci if d == 0 else nchunk - 1 - ci
            for p in range(np_):
                st = states[d][p]
                inter = lax.dot_general(qes[c, d][:, ksl[p]], st.astype(BF16), _NT, preferred_element_type=F32)
                dirs[d][4][0, csl[c], p * vw:(p + 1) * vw] = inter + intras[c, d, p]
                states[d][p] = decs[c, d][:, ksl[p]] * st + uts[c, d, p]
    for d in range(2):
        for p in range(np_):
            s_scr[d, p] = states[d][p]


def _gla(z, zlr, gw, gb, *, tg=512):
    B, T, _ = z.shape
    nb = T // tg
    kw = B_HEADS * B_K_DIM
    vw = B_HEADS * B_V_DIM
    q_blk = EVEN_BQ // kw
    v_blk = EVEN_BV // vw
    fwd = lambda b, i: (b, i)
    bwd = lambda b, i: (b, nb - 1 - i)

    def specs(order):
        return [
            pl.BlockSpec((1, tg, kw), lambda b, i: (*order(b, i), q_blk)),
            pl.BlockSpec((1, tg, kw), lambda b, i: (*order(b, i), q_blk + 1)),
            pl.BlockSpec((1, tg, vw), lambda b, i: (*order(b, i), v_blk)),
            pl.BlockSpec((1, tg, GATE_PAD), lambda b, i: (*order(b, i), 0)),
        ]

    kern = functools.partial(_gla_kernel, tg=tg)
    return pl.pallas_call(
        kern,
        grid=(B, nb),
        in_specs=specs(fwd) + specs(bwd) + [
            pl.BlockSpec((2, GATE_PAD, kw), lambda b, i: (0, 0, 0)),
            pl.BlockSpec((2, 1, kw), lambda b, i: (0, 0, 0)),
        ],
        out_specs=[
            pl.BlockSpec((1, tg, vw), lambda b, i: (b, i, 0)),
            pl.BlockSpec((1, tg, vw), lambda b, i: (b, nb - 1 - i, 0)),
        ],
        out_shape=[jax.ShapeDtypeStruct((B, T, vw), F32)] * 2,
        scratch_shapes=[pltpu.VMEM((2, B_HEADS // 2, 2 * B_V_DIM, 2 * B_K_DIM), F32)],
        compiler_params=_params(("parallel", "arbitrary")),
        name="gla",
    )(z, z, z, zlr, z, z, z, zlr, gw, gb)


def _head_rmsnorm(o, g, width):
    parts = []
    for h in range(o.shape[-1] // width):
        oh = o[:, h * width:(h + 1) * width]
        ms = jnp.mean(oh * oh, axis=-1, keepdims=True)
        parts.append(oh * lax.rsqrt(ms + EPS) * g[:, h * width:(h + 1) * width])
    return jnp.concatenate(parts, axis=-1)


def _mix_ffn_kernel(*refs, even, final_norm, ff_chunk):
    if even:
        a_ref, of_ref, ob_ref, gate_ref = refs[:4]
        refs = refs[4:]
    else:
        hf_ref, hb_ref, gate_ref = refs[:3]
        refs = refs[3:]
    gn_ref, x_ref, g1_ref, wo_ref, sh_ref, sc_ref, g_ref, g2_ref, wgu_ref, wd_ref, fg_ref, o_ref = refs
    if even:
        o = _head_rmsnorm(of_ref[0] + ob_ref[0], gn_ref[...], B_V_DIM)
        b_out = (o * _silu(gate_ref[0].astype(F32))).astype(BF16)
        mix = jnp.concatenate([a_ref[0], b_out], axis=-1)
    else:
        ht = _head_rmsnorm(hf_ref[0] + hb_ref[0], gn_ref[...], C_V_DIM)
        mix = (jax.nn.sigmoid(gate_ref[0].astype(F32)) * ht).astype(BF16)
    x1 = x_ref[0] + g1_ref[0] * jnp.dot(mix, wo_ref[...], preferred_element_type=F32)
    hb = _norm_mod(x1, g_ref[...], sh_ref[0], sc_ref[0]).astype(BF16)
    acts = []
    for c in range(D_FF // ff_chunk):
        gate = jnp.dot(hb, wgu_ref[:, c * ff_chunk:(c + 1) * ff_chunk], preferred_element_type=F32)
        up = jnp.dot(hb, wgu_ref[:, D_FF + c * ff_chunk:D_FF + (c + 1) * ff_chunk], preferred_element_type=F32)
        acts.append((_silu(gate) * up).astype(BF16))
    act = jnp.concatenate(acts, axis=-1)
    y = x1 + g2_ref[0] * jnp.dot(act, wd_ref[...], preferred_element_type=F32)
    if final_norm:
        ms = jnp.mean(y * y, axis=-1, keepdims=True)
        y = y * lax.rsqrt(ms + EPS) * fg_ref[...]
    o_ref[0] = y


def _mix_ffn(mixer_outs, z, gate_blk, gn, x, g1, w_out, sh, sc, g, g2, w_gu, w_down, final_g,
             *, even, final_norm, tm=512, ff_chunk=256):
    B, T, _ = x.shape
    tok = lambda b, i: (b, i, 0)
    per_b = lambda b, i: (b, 0, 0)
    const = lambda b, i: (0, 0)
    resident = functools.partial(pl.BlockSpec, index_map=const, pipeline_mode=pl.Buffered(1))
    nm = mixer_outs[0].shape[-1]
    in_specs = [pl.BlockSpec((1, tm, nm), tok) for _ in mixer_outs]
    in_specs += [
        pl.BlockSpec((1, tm, nm), lambda b, i: (b, i, gate_blk)),
        pl.BlockSpec((1, nm), const),
        pl.BlockSpec((1, tm, D_MODEL), tok),
        pl.BlockSpec((1, 1, D_MODEL), per_b),
        resident(w_out.shape),
        pl.BlockSpec((1, 1, D_MODEL), per_b),
        pl.BlockSpec((1, 1, D_MODEL), per_b),
        pl.BlockSpec((1, D_MODEL), const),
        pl.BlockSpec((1, 1, D_MODEL), per_b),
        resident(w_gu.shape),
        resident(w_down.shape),
        pl.BlockSpec((1, D_MODEL), const),
    ]
    kern = functools.partial(_mix_ffn_kernel, even=even, final_norm=final_norm, ff_chunk=ff_chunk)
    return pl.pallas_call(
        kern,
        grid=(B, T // tm),
        in_specs=in_specs,
        out_specs=pl.BlockSpec((1, tm, D_MODEL), tok),
        out_shape=jax.ShapeDtypeStruct((B, T, D_MODEL), F32),
        compiler_params=_params(("parallel", "parallel")),
        name=("mix_ffn_even" if even else "mix_ffn_odd") + ("_final" if final_norm else ""),
    )(*mixer_outs, z, gn, x, g1, w_out, sh, sc, g, g2, w_gu, w_down, final_g)


def _mlstm_kernel(gbr_ref, gbc_ref, qf_ref, kf_ref, vf_ref, gcf_ref, grf_ref, qb_ref, kb_ref, vb_ref, gcb_ref,
                  grb_ref, of_ref, ob_ref, c_scr, n_scr, m_scr, *, tg):
    L = C_CHUNK
    nchunk = tg // L
    dk, dv = C_QK_DIM, C_V_DIM

    @pl.when(pl.program_id(1) == 0)
    def _():
        c_scr[...] = jnp.zeros(c_scr.shape, F32)
        n_scr[...] = jnp.zeros(n_scr.shape, F32)
        m_scr[...] = jnp.zeros(m_scr.shape, F32)

    tt = lax.broadcasted_iota(jnp.int32, (L, L), 0)
    ss = lax.broadcasted_iota(jnp.int32, (L, L), 1)

    dirs = ((qf_ref, kf_ref, vf_ref, gcf_ref, grf_ref, of_ref),
            (qb_ref, kb_ref, vb_ref, gcb_ref, grb_ref, ob_ref))
    keeps = (ss <= tt, ss >= tt)
    tri_col = [keeps[0].astype(BF16), keeps[1].astype(BF16)]
    tri_row = [keeps[1].astype(BF16), keeps[0].astype(BF16)]
    glane = lax.broadcasted_iota(jnp.int32, (1, GATE_PAD), 1)
    f_lane = jnp.logical_and((glane & C_HEADS) != 0, glane < 4 * C_HEADS)
    f_row = (lax.broadcasted_iota(jnp.int32, (4 * C_HEADS, 1), 0) & C_HEADS) != 0
    csl = [slice(c * L, (c + 1) * L) for c in range(nchunk)]
    items = [(c, d) for c in range(nchunk) for d in range(2)]
    chains = [(c, d, h) for c, d in items for h in range(C_HEADS)]
    ii = {(d, h): (2 * d) * C_HEADS + h for d in range(2) for h in range(C_HEADS)}
    fi = {(d, h): (2 * d + 1) * C_HEADS + h for d in range(2) for h in range(C_HEADS)}

    Gc = []
    for d in range(2):
        g = dirs[d][3][0] + gbr_ref[...]
        Gc.append(jnp.where(f_lane, _log_sigmoid(g), g))
    Gr = {}
    for c, d in items:
        g = dirs[d][4][0, c] + gbc_ref[...]
        Gr[c, d] = jnp.where(f_row, _log_sigmoid(g), g)
    Fc = {(c, d): _dot_exact_lhs(tri_col[d], Gc[d][csl[c]]) for c, d in items}
    Fr = {(c, d): _dot_exact_rhs(Gr[c, d], tri_row[d]) for c, d in items}
    F_row = {(c, d, h): Fr[c, d][fi[d, h]:fi[d, h] + 1, :] for c, d, h in chains}
    i_row = {(c, d, h): Gr[c, d][ii[d, h]:ii[d, h] + 1, :] for c, d, h in chains}
    F_col = {(c, d, h): Fc[c, d][:, fi[d, h]:fi[d, h] + 1] for c, d, h in chains}
    i_col = {(c, d, h): Gc[d][csl[c], ii[d, h]:ii[d, h] + 1] for c, d, h in chains}
    FL = {(c, d, h): F_row[c, d, h][:, L - 1:L] if d == 0 else F_row[c, d, h][:, 0:1] for c, d, h in chains}
    wl_max = {k_: jnp.max(FL[k_] - F_row[k_] + i_row[k_], axis=1, keepdims=True) for k_ in chains}
    m_old, m_new = {}, {}
    m_cur = {(d, h): m_scr[d, h] for d in range(2) for h in range(C_HEADS)}
    for ci in range(nchunk):
        for d in range(2):
            c = ci if d == 0 else nchunk - 1 - ci
            for h in range(C_HEADS):
                m_old[c, d, h] = m_cur[d, h]
                m_new[c, d, h] = jnp.maximum(FL[c, d, h] + m_cur[d, h], wl_max[c, d, h])
                m_cur[d, h] = m_new[c, d, h]
    Dm = {k_: jnp.where(keeps[k_[1]], F_col[k_] - F_row[k_] + i_row[k_], -jnp.inf) for k_ in chains}
    a_col = {k_: F_col[k_] + m_old[k_] for k_ in chains}
    mt = {k_: jnp.maximum(a_col[k_], jnp.max(Dm[k_], axis=1, keepdims=True)) for k_ in chains}
    qc = {(c, d, h): dirs[d][0][0, csl[c], h * dk:(h + 1) * dk] for c, d, h in chains}
    kf = {(c, d, h): dirs[d][1][0, csl[c], h * dk:(h + 1) * dk].astype(F32) * (dk ** -0.5) for c, d, h in chains}
    vc = {(c, d, h): dirs[d][2][0, csl[c], h * dv:(h + 1) * dv] for c, d, h in chains}
    qk = {k_: lax.dot_general(qc[k_], kf[k_].astype(BF16), _NT, preferred_element_type=F32) for k_ in chains}
    W = {k_: jnp.exp(Dm[k_] - mt[k_]) * qk[k_] for k_ in chains}
    inter = {k_: jnp.exp(a_col[k_] - mt[k_]) for k_ in chains}
    w_sum = {k_: jnp.sum(W[k_], axis=1, keepdims=True) for k_ in chains}
    e_mt = {k_: jnp.exp(-mt[k_]) for k_ in chains}
    Wv = {k_: jnp.dot(W[k_].astype(BF16), vc[k_], preferred_element_type=F32) for k_ in chains}
    kw = {k_: kf[k_] * jnp.exp(FL[k_] - F_col[k_] + i_col[k_] - m_new[k_]) for k_ in chains}
    kv = {k_: lax.dot_general(kw[k_].astype(BF16), vc[k_], _TN, preferred_element_type=F32) for k_ in chains}
    k_sum = {k_: jnp.sum(kw[k_], axis=0, keepdims=True) for k_ in chains}
    dec = {k_: jnp.exp(FL[k_] + m_old[k_] - m_new[k_]) for k_ in chains}
    Cs = {(d, h): c_scr[d, h] for d in range(2) for h in range(C_HEADS)}
    ns = {(d, h): n_scr[d, h] for d in range(2) for h in range(C_HEADS)}
    for ci in range(nchunk):
        for d in range(2):
            c = ci if d == 0 else nchunk - 1 - ci
            for h in range(C_HEADS):
                k_ = (c, d, h)
                C, n = Cs[d, h], ns[d, h]
                num = inter[k_] * jnp.dot(qc[k_], C.astype(BF16), preferred_element_type=F32) + Wv[k_]
                qn = jnp.sum(qc[k_].astype(F32) * n, axis=1, keepdims=True)
                den = inter[k_] * qn + w_sum[k_]
                dirs[d][5][0, csl[c], h * dv:(h + 1) * dv] = num / jnp.maximum(jnp.abs(den), e_mt[k_])
                Cs[d, h] = dec[k_] * C + kv[k_]
                ns[d, h] = dec[k_] * n + k_sum[k_]
    for d in range(2):
        for h in range(C_HEADS):
            c_scr[d, h] = Cs[d, h]
            n_scr[d, h] = ns[d, h]
            m_scr[d, h] = m_cur[d, h]


def _mlstm(z, gcol, grow, gate_b, *, tg=256):
    B, T, _ = z.shape
    nb = T // tg
    qw = C_HEADS * C_QK_DIM
    vw = C_HEADS * C_V_DIM
    ng = grow.shape[2]
    fwd = lambda b, i: (b, i)
    bwd = lambda b, i: (b, nb - 1 - i)

    def specs(order):
        return [
            pl.BlockSpec((1, tg, qw), lambda b, i: (*order(b, i), 0)),
            pl.BlockSpec((1, tg, qw), lambda b, i: (*order(b, i), 1)),
            pl.BlockSpec((1, tg, vw), lambda b, i: (*order(b, i), 1)),
            pl.BlockSpec((1, tg, GATE_PAD), lambda b, i: (*order(b, i), 0)),
            pl.BlockSpec((1, tg // C_CHUNK, ng, C_CHUNK), lambda b, i: (*order(b, i), 0, 0)),
        ]

    kern = functools.partial(_mlstm_kernel, tg=tg)
    return pl.pallas_call(
        kern,
        grid=(B, nb),
        in_specs=[pl.BlockSpec((1, GATE_PAD), lambda b, i: (0, 0)),
                  pl.BlockSpec((ng, 1), lambda b, i: (0, 0))] + specs(fwd) + specs(bwd),
        out_specs=[
            pl.BlockSpec((1, tg, vw), lambda b, i: (b, i, 0)),
            pl.BlockSpec((1, tg, vw), lambda b, i: (b, nb - 1 - i, 0)),
        ],
        out_shape=[jax.ShapeDtypeStruct((B, T, vw), F32)] * 2,
        scratch_shapes=[
            pltpu.VMEM((2, C_HEADS, C_QK_DIM, C_V_DIM), F32),
            pltpu.VMEM((2, C_HEADS, 1, C_QK_DIM), F32),
            pltpu.VMEM((2, C_HEADS, 1, 1), F32),
        ],
        compiler_params=_params(("parallel", "arbitrary")),
        name="mlstm",
    )(_pad_cols(gate_b[None, :], GATE_PAD), gate_b[:, None], z, z, z, gcol, grow, z, z, z, gcol, grow)


def _pad_cols(w, n):
    return jnp.pad(w, ((0, 0), (0, n - w.shape[1])))


def _trunk(x, mod, p):
    for i in range(DEPTH):
        sh1, sc1, g1, sh2, sc2, g2 = [m[:, None, :] for m in jnp.split(mod[i], 6, axis=-1)]
        if i % 2 == 0:
            j = i // 2
            lam_init = 0.8 - 0.6 * math.exp(-0.3 * i)
            z, zlr, vt = _inproj(x, sh1, sc1, p["norm1_g"][i], p["even_w_main"][j], p["even_w_lr"][j],
                                 p["even_w_vt"][j], with_rows=False)
            a = _diff_attention(z, vt, p["even_lam"][j], p["even_sub_g"][j], lam_init)
            of, ob = _gla(z, zlr, p["even_gk_w"][j], p["even_gk_b"][j])
            mixer, gate_blk, gn, w_out = (a, of, ob), EVEN_BG // a.shape[-1], p["even_gla_norm_g"][j], p["even_w_out"][j]
        else:
            j = i // 2
            z, gcol, grow = _inproj(x, sh1, sc1, p["norm1_g"][i], p["odd_w_main"][j], p["odd_w_gate"][j],
                                    p["odd_w_gate_t"][j], with_rows=True, chunk=C_CHUNK)
            hf, hb = _mlstm(z, gcol, grow, p["odd_gate_b"][j])
            mixer, gate_blk, gn, w_out = (hf, hb), ODD_CO // hf.shape[-1], p["odd_norm_g"][j], p["odd_w_out"][j]
        x = _mix_ffn(mixer, z, gate_blk, gn, x, g1, w_out, sh2, sc2, p["norm2_g"][i], g2, p["ffn_w_gu"][i],
                     p["ffn_w_down"][i], p["final_g"], even=(i % 2 == 0), final_norm=(i == DEPTH - 1))
    return x


def kernel(x_prompt, x_sample, c_prompt, c_sample, w_mod, b_mod, norm1_g, norm2_g, even_w_in, even_lam_q1, even_lam_k1, even_lam_q2, even_lam_k2, even_attn_sub_g, even_gk_w_f, even_gk_b_f, even_gk_w_b, even_gk_b_b, even_gla_norm_g, even_w_out, odd_w_in, odd_gate_b, odd_norm_g, odd_w_out, ffn_w_gu, ffn_w_down, final_g):
    n_even = even_w_in.shape[0]
    n_odd = odd_w_in.shape[0]
    r = B_GATE_RANK
    kw = B_HEADS * B_K_DIM
    gk_w = jnp.zeros((n_even, 2, GATE_PAD, kw), F32)
    gk_w = gk_w.at[:, 0, 0:r].set(even_gk_w_f).at[:, 1, r:2 * r].set(even_gk_w_b)
    p = {
        "norm1_g": norm1_g[:, None, :],
        "norm2_g": norm2_g[:, None, :],
        "final_g": final_g[None, :],
        "even_w_main": jnp.concatenate([even_w_in[:, :, :EVEN_AV], even_w_in[:, :, EVEN_REST:EVEN_MAIN]],
                                       axis=-1).astype(BF16),
        "even_w_vt": jnp.swapaxes(even_w_in[:, :, EVEN_AV:EVEN_REST], 1, 2).astype(BF16),
        "even_w_lr": jnp.stack([_pad_cols(even_w_in[j, :, EVEN_MAIN:], GATE_PAD) for j in range(n_even)]),
        "even_lam": jnp.stack([even_lam_q1, even_lam_k1, even_lam_q2, even_lam_k2], axis=1),
        "even_sub_g": even_attn_sub_g[:, :, None],
        "even_gk_w": gk_w,
        "even_gk_b": jnp.stack([even_gk_b_f, even_gk_b_b], axis=1)[:, :, None, :],
        "even_gla_norm_g": even_gla_norm_g[:, None, :],
        "even_w_out": even_w_out.astype(BF16),
        "odd_w_main": odd_w_in[:, :, :ODD_MAIN].astype(BF16),
        "odd_w_gate": jnp.stack([_pad_cols(odd_w_in[j, :, ODD_MAIN:], GATE_PAD) for j in range(n_odd)]),
        "odd_w_gate_t": jnp.swapaxes(odd_w_in[:, :, ODD_MAIN:], 1, 2),
        "odd_gate_b": odd_gate_b,
        "odd_norm_g": odd_norm_g[:, None, :],
        "odd_w_out": odd_w_out.astype(BF16),
        "ffn_w_gu": ffn_w_gu.astype(BF16),
        "ffn_w_down": ffn_w_down.astype(BF16),
    }
    nbp = c_prompt.shape[0]
    mod = _modulation(jnp.concatenate([c_prompt, c_sample], axis=0), w_mod, b_mod)
    y_prompt = _trunk(x_prompt, mod[:, :nbp], p)
    y_sample = _trunk(x_sample, mod[:, nbp:], p)
    return (y_prompt, y_sample)
```

```python
import functools
import math

import jax
import jax.numpy as jnp
from jax import lax
from jax.experimental import pallas as pl
from jax.experimental.pallas import tpu as pltpu

F32 = jnp.float32
BF16 = jnp.bfloat16
HIGHEST = lax.Precision.HIGHEST

D_MODEL = 1024
DEPTH = 2
EPS = 1e-6
A_HEADS = 4
A_QK_DIM = 64
A_V_DIM = 128
B_HEADS = 4
B_K_DIM = 64
B_V_DIM = 128
B_GATE_RANK = 16
B_GATE_NORM = 16.0
B_CHUNK = 64
C_HEADS = 4
C_QK_DIM = 128
C_V_DIM = 256
C_CHUNK = 64
D_FF = 2816
EVEN_AV = 1024
EVEN_REST = 1536
EVEN_MAIN = 3072
EVEN_AK = 512
EVEN_BQ = 1536
EVEN_BV = 2048
EVEN_BG = 2560
ODD_MAIN = 3072
ODD_CO = 2048
GATE_PAD = 128
ONES_ROWS = 16
LOG2E = 1.4426950408889634
ATTN_SUB = 256
ATTN_AHEAD = 2

VMEM_LIMIT = 56 * 1024 * 1024

_NT = (((1,), (1,)), ((), ()))
_TN = (((0,), (0,)), ((), ()))


def _log_sigmoid(x):
    return jnp.minimum(x, 0.0) - jnp.log1p(jnp.exp(-jnp.abs(x)))


def _silu(x):
    return x * jax.nn.sigmoid(x)


def _split_bf16(x, terms):
    parts = []
    for _ in range(terms - 1):
        hi = x.astype(BF16)
        parts.append(hi)
        x = x - hi.astype(F32)
    parts.append(x.astype(BF16))
    return parts


def _dot_exact_lhs(a_bf16, x):
    return sum(jnp.dot(a_bf16, part, preferred_element_type=F32) for part in _split_bf16(x, 3))


def _dot_exact_rhs(x, a_bf16):
    return sum(jnp.dot(part, a_bf16, preferred_element_type=F32) for part in _split_bf16(x, 3))


def _dot_split2(x, w, dims=(((1,), (0,)), ((), ()))):
    xh, xl = _split_bf16(x, 2)
    wh, wl = _split_bf16(w, 2)
    dot = functools.partial(lax.dot_general, dimension_numbers=dims, preferred_element_type=F32)
    return dot(xh, wh) + dot(xh, wl) + dot(xl, wh)


def _params(sem, flags=None):
    return pltpu.CompilerParams(dimension_semantics=sem, vmem_limit_bytes=VMEM_LIMIT, flags=flags)


def _mod_kernel(c_ref, w_ref, b_ref, o_ref):
    s = _silu(c_ref[...])
    o_ref[0] = jnp.dot(s, w_ref[0], precision=HIGHEST, preferred_element_type=F32) + b_ref[0]


def _modulation(c_all, w_mod, b_mod):
    nb = c_all.shape[0]
    tn = 1536
    return pl.pallas_call(
        _mod_kernel,
        grid=(DEPTH, 6 * D_MODEL // tn),
        in_specs=[
            pl.BlockSpec((nb, D_MODEL), lambda i, j: (0, 0)),
            pl.BlockSpec((1, D_MODEL, tn), lambda i, j: (i, 0, j)),
            pl.BlockSpec((1, 1, tn), lambda i, j: (i, 0, j)),
        ],
        out_specs=pl.BlockSpec((1, nb, tn), lambda i, j: (i, 0, j)),
        out_shape=jax.ShapeDtypeStruct((DEPTH, nb, 6 * D_MODEL), F32),
        compiler_params=_params(("parallel", "parallel")),
        name="modulation",
    )(c_all, w_mod, b_mod.reshape(DEPTH, 1, 6 * D_MODEL))


def _norm_mod(x, g, sh, sc):
    ms = jnp.mean(x * x, axis=-1, keepdims=True)
    y = x * lax.rsqrt(ms + EPS) * g
    return y * (1.0 + sc) + sh


def _inproj_kernel(x_ref, sh_ref, sc_ref, g_ref, w_ref, wg_ref, wt_ref, *rest, n_main, n_col, chunk, with_rows):
    if with_rows:
        z_ref, zg_ref, zt_ref = rest
    else:
        kf_ref, z_ref, zg_ref, zt_ref = rest
    h = _norm_mod(x_ref[0], g_ref[...], sh_ref[0], sc_ref[0])
    hb = h.astype(BF16)
    tm = h.shape[0]
    for j in range(n_main // n_col):
        zc = jnp.dot(hb, w_ref[:, j * n_col:(j + 1) * n_col], preferred_element_type=F32)
        if with_rows or j == 0:
            z_ref[0, :, j * n_col:(j + 1) * n_col] = zc.astype(BF16)
        elif j == 1:
            hw = 2 * A_QK_DIM
            dup = jnp.concatenate([zc[:, (i // 2) * hw:(i // 2 + 1) * hw] for i in range(2 * A_HEADS)], axis=1)
            pos = (lax.broadcasted_iota(jnp.int32, (tm, 1), 0) & (ATTN_SUB - 1)).astype(F32)
            feat = kf_ref[1:2, :] + kf_ref[2:3, :] * pos
            z_ref[0, :, n_col:3 * n_col] = jnp.where(kf_ref[0:1, :] > 0.0, dup, feat).astype(BF16)
        else:
            z_ref[0, :, (j + 1) * n_col:(j + 2) * n_col] = zc.astype(BF16)
    zg_ref[0] = _dot_split2(h, wg_ref[...])
    if with_rows:
        for c in range(tm // chunk):
            zt_ref[0, c] = _dot_split2(wt_ref[...], h[c * chunk:(c + 1) * chunk], _NT)
    else:
        for c in range(h.shape[0] // ATTN_SUB):
            zt_ref[0, c] = lax.dot_general(wt_ref[...], hb[c * ATTN_SUB:(c + 1) * ATTN_SUB], _NT,
                                           preferred_element_type=F32).astype(BF16)


def _inproj(x, sh, sc, g, w_main, w_gate, w_t, k_feat=None, *, with_rows, tm=512, chunk=64):
    B, T, _ = x.shape
    n_main = w_main.shape[1]
    n_col = 512
    n_z = n_main if with_rows else n_main + n_col
    kern = functools.partial(_inproj_kernel, n_main=n_main, n_col=n_col, chunk=chunk, with_rows=with_rows)
    in_specs = [
        pl.BlockSpec((1, tm, D_MODEL), lambda b, i: (b, i, 0)),
        pl.BlockSpec((1, 1, D_MODEL), lambda b, i: (b, 0, 0)),
        pl.BlockSpec((1, 1, D_MODEL), lambda b, i: (b, 0, 0)),
        pl.BlockSpec((1, D_MODEL), lambda b, i: (0, 0)),
        pl.BlockSpec((D_MODEL, n_main), lambda b, i: (0, 0)),
        pl.BlockSpec((D_MODEL, GATE_PAD), lambda b, i: (0, 0)),
    ]
    out_specs = [
        pl.BlockSpec((1, tm, n_z), lambda b, i: (b, i, 0)),
        pl.BlockSpec((1, tm, GATE_PAD), lambda b, i: (b, i, 0)),
    ]
    out_shape = [
        jax.ShapeDtypeStruct((B, T, n_z), BF16),
        jax.ShapeDtypeStruct((B, T, GATE_PAD), F32),
    ]
    args = [x, sh, sc, g, w_main, w_gate, w_t]
    nt = w_t.shape[0]
    in_specs.append(pl.BlockSpec((nt, D_MODEL), lambda b, i: (0, 0)))
    if with_rows:
        out_specs.append(pl.BlockSpec((1, tm // chunk, nt, chunk), lambda b, i: (b, i, 0, 0)))
        out_shape.append(jax.ShapeDtypeStruct((B, T // chunk, nt, chunk), F32))
    else:
        args.append(k_feat)
        in_specs.append(pl.BlockSpec(k_feat.shape, lambda b, i: (0, 0)))
        out_specs.append(pl.BlockSpec((1, tm // ATTN_SUB, nt, ATTN_SUB), lambda b, i: (b, i, 0, 0)))
        out_shape.append(jax.ShapeDtypeStruct((B, T // ATTN_SUB, nt, ATTN_SUB), BF16))
    return pl.pallas_call(
        kern,
        grid=(B, T // tm),
        in_specs=in_specs,
        out_specs=out_specs,
        out_shape=out_shape,
        compiler_params=_params(("parallel", "parallel")),
        name="inproj_rows" if with_rows else "inproj",
    )(*args)


def _attn_kernel(slope_ref, q_ref, ka_ref, kb_ref, vt_ref, lamp_ref, subg_ref, o_ref, q_scr, *, n_sub, lam_init):
    h = pl.program_id(1)
    qi = pl.program_id(2)
    tq = sk = ATTN_SUB
    dv = A_V_DIM
    hw = 2 * A_QK_DIM
    slope2 = slope_ref[h, 0] + slope_ref[h, 1] + slope_ref[h, 2]

    lane = lax.broadcasted_iota(jnp.int32, (1, hw), 1)
    qs = (q_ref[0].astype(F32) * (A_QK_DIM ** -0.5 * LOG2E)).astype(BF16)
    c_off = lax.broadcasted_iota(jnp.int32, (tq, hw), 0).astype(F32)
    for c in range(2):
        own = (lane < A_QK_DIM) if c == 0 else (lane >= A_QK_DIM)
        base = (1 - c) * A_QK_DIM
        q_feat = jnp.zeros((tq, hw), F32)
        for x in range(3):
            q_feat = jnp.where(lane == base + x, slope_ref[h, x], q_feat)
            q_feat = jnp.where(lane == base + 3 + x, -c_off, q_feat)
        for v, sign in enumerate((1.0, -1.0, 0.0)):
            q_scr[v, c] = jnp.where(own, qs, (sign * q_feat).astype(BF16))
    rel = (lax.broadcasted_iota(jnp.int32, (sk, tq), 0)
           - lax.broadcasted_iota(jnp.int32, (sk, tq), 1)).astype(F32)
    diag_bias = -slope2 * jnp.abs(rel)
    ones = jnp.ones((ONES_ROWS, sk), BF16)
    k_refs = (ka_ref, kb_ref)

    def scores(j):
        if j == 0:
            idx, const, v = qi, jnp.zeros((), F32), 2
        else:
            idx = qi + j
            idx = jnp.where(idx >= n_sub, idx - n_sub, idx)
            before = idx < qi
            const = jnp.where(before, 1.0, -1.0) * slope2 * ((idx - qi) * sk).astype(F32)
            v = jnp.where(before, 0, 1)
        rows = pl.ds(pl.multiple_of(idx * sk, sk), sk)
        s = [lax.dot_general(k_refs[c][0, rows, :], q_scr[v, c], _NT, preferred_element_type=F32)
             for c in range(2)]
        if j == 0:
            s = [s_c + diag_bias for s_c in s]
        s = jnp.concatenate(s, axis=1)
        return s, jnp.max(s, axis=0, keepdims=True) + const, const, idx

    m = jnp.full((1, 2 * tq), -jnp.inf, F32)
    acc = jnp.zeros((dv + ONES_ROWS, 2 * tq), F32)
    def accumulate(acc, alpha, p, idx):
        vt1 = jnp.concatenate([vt_ref[0, idx], ones], axis=0)
        return alpha * acc + jnp.dot(vt1, p, preferred_element_type=F32)

    ahead = [scores(j) for j in range(min(ATTN_AHEAD, n_sub))]
    pending = None
    for j in range(n_sub):
        s, m_loc, const, idx = ahead.pop(0)
        if j + ATTN_AHEAD < n_sub:
            ahead.append(scores(j + ATTN_AHEAD))
        m_new = jnp.maximum(m, m_loc)
        alpha = jnp.exp2(m - m_new)
        p = jnp.exp2(s - (m_new - const)).astype(BF16)
        m = m_new
        if pending is not None:
            acc = accumulate(acc, *pending)
        pending = (alpha, p, idx)
    acc = accumulate(acc, *pending)

    lp = lamp_ref[...]
    lam = (jnp.exp(jnp.sum(lp[0:1] * lp[1:2], axis=-1, keepdims=True))
           - jnp.exp(jnp.sum(lp[2:3] * lp[3:4], axis=-1, keepdims=True)) + lam_init)
    on = acc[:dv] / acc[dv:dv + 1]
    o = on[:, :tq] - lam * on[:, tq:]
    ms = jnp.mean(o * o, axis=0, keepdims=True)
    y = o * lax.rsqrt(ms + EPS) * subg_ref[...] * (1.0 - lam_init)
    o_ref[0] = y.T.astype(o_ref.dtype)


def _alibi_slope_terms():
    slope2 = jnp.exp2(-8.0 * jnp.arange(1, A_HEADS + 1, dtype=F32) / A_HEADS) * LOG2E
    s0 = slope2.astype(BF16).astype(F32)
    s1 = (slope2 - s0).astype(BF16).astype(F32)
    s2 = (slope2 - s0 - s1).astype(BF16).astype(F32)
    return jnp.stack([s0, s1, s2], axis=1)


def _alibi_key_features():
    hw = 2 * A_QK_DIM
    terms = _alibi_slope_terms()
    lane = jnp.arange(hw)
    rows = []
    for h in range(A_HEADS):
        for c in range(2):
            base = (1 - c) * A_QK_DIM
            keep = ((lane < A_QK_DIM) if c == 0 else (lane >= A_QK_DIM)).astype(F32)
            const = jnp.zeros((hw,), F32).at[base + 3:base + 6].set(terms[h])
            pos = jnp.zeros((hw,), F32).at[base:base + 3].set(1.0)
            rows.append(jnp.stack([keep, const, pos]))
    return jnp.concatenate(rows, axis=1)


def _diff_attention(z, vt, lam_params, sub_g_col, lam_init):
    B, T, _ = z.shape
    n_sub = T // ATTN_SUB
    slopes = _alibi_slope_terms()
    kern = functools.partial(_attn_kernel, n_sub=n_sub, lam_init=lam_init)
    hw = 2 * A_QK_DIM
    return pl.pallas_call(
        kern,
        grid=(B, A_HEADS, n_sub),
        in_specs=[
            pl.BlockSpec(memory_space=pltpu.SMEM),
            pl.BlockSpec((1, ATTN_SUB, hw), lambda b, h, i: (b, i, h)),
            pl.BlockSpec((1, T, hw), lambda b, h, i: (b, 0, EVEN_AK // hw + 2 * h)),
            pl.BlockSpec((1, T, hw), lambda b, h, i: (b, 0, EVEN_AK // hw + 2 * h + 1)),
            pl.BlockSpec((1, n_sub, A_V_DIM, ATTN_SUB), lambda b, h, i: (b, 0, h, 0)),
            pl.BlockSpec((4, A_QK_DIM), lambda b, h, i: (0, 0)),
            pl.BlockSpec((A_V_DIM, 1), lambda b, h, i: (0, 0)),
        ],
        out_specs=pl.BlockSpec((1, ATTN_SUB, A_V_DIM), lambda b, h, i: (b, i, h)),
        out_shape=jax.ShapeDtypeStruct((B, T, A_HEADS * A_V_DIM), BF16),
        scratch_shapes=[pltpu.VMEM((3, 2, ATTN_SUB, hw), BF16)],
        compiler_params=_params(("parallel", "parallel", "parallel")),
        name="diff_attention",
    )(slopes, z, z, z, vt, lam_params, sub_g_col)


def _gla_kernel(qf_ref, kf_ref, vf_ref, lrf_ref, qb_ref, kb_ref, vb_ref, lrb_ref, gw_ref, gb_ref,
                of_ref, ob_ref, s_scr, *, tg):
    L = B_CHUNK
    nchunk = tg // L
    pw = 2 * B_K_DIM
    vw = 2 * B_V_DIM

    @pl.when(pl.program_id(1) == 0)
    def _():
        s_scr[...] = jnp.zeros(s_scr.shape, F32)

    tt = lax.broadcasted_iota(jnp.int32, (L, L), 0)
    ss = lax.broadcasted_iota(jnp.int32, (L, L), 1)
    lane = lax.broadcasted_iota(jnp.int32, (1, pw), 1)
    srow = lax.broadcasted_iota(jnp.int32, (vw, pw), 0)
    slane = lax.broadcasted_iota(jnp.int32, (vw, pw), 1)
    same_head = (srow >= B_V_DIM) == (slane >= B_K_DIM)
    zero_b = jnp.zeros((), BF16)

    dirs = ((qf_ref, kf_ref, vf_ref, lrf_ref, of_ref), (qb_ref, kb_ref, vb_ref, lrb_ref, ob_ref))
    keeps = (ss <= tt, ss >= tt)
    tris = [kp.astype(BF16) for kp in keeps]
    lgs = []
    for d in range(2):
        pre = _dot_split2(dirs[d][3][0], gw_ref[d]) + gb_ref[d]
        lgs.append(_log_sigmoid(pre) * (1.0 / B_GATE_NORM))
    np_ = B_HEADS // 2
    states = [[s_scr[d, p] for p in range(np_)] for d in range(2)]
    sels = (lane < B_K_DIM, lane >= B_K_DIM)
    csl = [slice(c * L, (c + 1) * L) for c in range(nchunk)]
    ksl = [slice(p * pw, (p + 1) * pw) for p in range(np_)]
    items = [(c, d) for c in range(nchunk) for d in range(2)]
    chains = [(c, d, p) for c, d in items for p in range(np_)]
    bs = {(c, d): _dot_exact_lhs(tris[d], lgs[d][csl[c]]) for c, d in items}
    b_ends = {(c, d): bs[c, d][L - 1:L] if d == 0 else bs[c, d][0:1] for c, d in items}
    qcs = {(c, d): dirs[d][0][0, csl[c], :].astype(F32) for c, d in items}
    kcs = {(c, d): dirs[d][1][0, csl[c], :].astype(F32) for c, d in items}
    qes = {i: (qcs[i] * (B_K_DIM ** -0.5) * jnp.exp(bs[i])).astype(BF16) for i in items}
    kes = {i: (kcs[i] * jnp.exp(-bs[i])).astype(BF16) for i in items}
    kds = {i: (kcs[i] * jnp.exp(b_ends[i] - bs[i])).astype(BF16) for i in items}
    decs = {i: jnp.exp(b_ends[i]) for i in items}
    v_ps = {(c, d, p): dirs[d][2][0, csl[c], p * vw:(p + 1) * vw] for c, d, p in chains}
    atts = {(c, d, p, hh): lax.dot_general(jnp.where(sels[hh], qes[c, d][:, ksl[p]], zero_b), kes[c, d][:, ksl[p]],
                                           _NT, preferred_element_type=F32)
            for c, d, p in chains for hh in range(2)}
    atts = {key: jnp.where(keeps[key[1]], a, 0.0).astype(BF16) for key, a in atts.items()}
    intras = {(c, d, p): jnp.concatenate(
        [jnp.dot(atts[c, d, p, hh], v_ps[c, d, p][:, hh * B_V_DIM:(hh + 1) * B_V_DIM],
                 preferred_element_type=F32) for hh in range(2)], axis=-1) for c, d, p in chains}
    uts = {(c, d, p): jnp.where(same_head, lax.dot_general(v_ps[c, d, p], kds[c, d][:, ksl[p]], _TN,
                                                           preferred_element_type=F32), 0.0)
           for c, d, p in chains}
    for ci in range(nchunk):
        for d in range(2):
            c = ci if d == 0 else nchunk - 1 - ci
            for p in range(np_):
                st = states[d][p]
                inter = lax.dot_general(qes[c, d][:, ksl[p]], st.astype(BF16), _NT, preferred_element_type=F32)
                dirs[d][4][0, csl[c], p * vw:(p + 1) * vw] = inter + intras[c, d, p]
                states[d][p] = decs[c, d][:, ksl[p]] * st + uts[c, d, p]
    for d in range(2):
        for p in range(np_):
            s_scr[d, p] = states[d][p]


def _gla(z, zlr, gw, gb, *, tg=512):
    B, T, _ = z.shape
    nb = T // tg
    kw = B_HEADS * B_K_DIM
    vw = B_HEADS * B_V_DIM
    q_blk = EVEN_BQ // kw
    v_blk = EVEN_BV // vw
    fwd = lambda b, i: (b, i)
    bwd = lambda b, i: (b, nb - 1 - i)

    def specs(order):
        return [
            pl.BlockSpec((1, tg, kw), lambda b, i: (*order(b, i), q_blk)),
            pl.BlockSpec((1, tg, kw), lambda b, i: (*order(b, i), q_blk + 1)),
            pl.BlockSpec((1, tg, vw), lambda b, i: (*order(b, i), v_blk)),
            pl.BlockSpec((1, tg, GATE_PAD), lambda b, i: (*order(b, i), 0)),
        ]

    kern = functools.partial(_gla_kernel, tg=tg)
    return pl.pallas_call(
        kern,
        grid=(B, nb),
        in_specs=specs(fwd) + specs(bwd) + [
            pl.BlockSpec((2, GATE_PAD, kw), lambda b, i: (0, 0, 0)),
            pl.BlockSpec((2, 1, kw), lambda b, i: (0, 0, 0)),
        ],
        out_specs=[
            pl.BlockSpec((1, tg, vw), lambda b, i: (b, i, 0)),
            pl.BlockSpec((1, tg, vw), lambda b, i: (b, nb - 1 - i, 0)),
        ],
        out_shape=[jax.ShapeDtypeStruct((B, T, vw), F32)] * 2,
        scratch_shapes=[pltpu.VMEM((2, B_HEADS // 2, 2 * B_V_DIM, 2 * B_K_DIM), F32)],
        compiler_params=_params(("parallel", "arbitrary")),
        name="gla",
    )(z, z, z, zlr, z, z, z, zlr, gw, gb)


def _head_rmsnorm(o, g, width):
    parts = []
    for h in range(o.shape[-1] // width):
        oh = o[:, h * width:(h + 1) * width]
        ms = jnp.mean(oh * oh, axis=-1, keepdims=True)
        parts.append(oh * lax.rsqrt(ms + EPS) * g[:, h * width:(h + 1) * width])
    return jnp.concatenate(parts, axis=-1)


def _mix_ffn_kernel(*refs, even, final_norm, ff_chunk):
    if even:
        a_ref, of_ref, ob_ref, gate_ref = refs[:4]
        refs = refs[4:]
    else:
        hf_ref, hb_ref, gate_ref = refs[:3]
        refs = refs[3:]
    gn_ref, x_ref, g1_ref, wo_ref, sh_ref, sc_ref, g_ref, g2_ref, wgu_ref, wd_ref, fg_ref, o_ref = refs
    if even:
        o = _head_rmsnorm(of_ref[0] + ob_ref[0], gn_ref[...], B_V_DIM)
        b_out = (o * _silu(gate_ref[0].astype(F32))).astype(BF16)
        mix = jnp.concatenate([a_ref[0], b_out], axis=-1)
    else:
        ht = _head_rmsnorm(hf_ref[0] + hb_ref[0], gn_ref[...], C_V_DIM)
        mix = (jax.nn.sigmoid(gate_ref[0].astype(F32)) * ht).astype(BF16)
    x1 = x_ref[0] + g1_ref[0] * jnp.dot(mix, wo_ref[...], preferred_element_type=F32)
    hb = _norm_mod(x1, g_ref[...], sh_ref[0], sc_ref[0]).astype(BF16)
    acts = []
    for c in range(D_FF // ff_chunk):
        gate = jnp.dot(hb, wgu_ref[:, c * ff_chunk:(c + 1) * ff_chunk], preferred_element_type=F32)
        up = jnp.dot(hb, wgu_ref[:, D_FF + c * ff_chunk:D_FF + (c + 1) * ff_chunk], preferred_element_type=F32)
        acts.append((_silu(gate) * up).astype(BF16))
    act = jnp.concatenate(acts, axis=-1)
    y = x1 + g2_ref[0] * jnp.dot(act, wd_ref[...], preferred_element_type=F32)
    if final_norm:
        ms = jnp.mean(y * y, axis=-1, keepdims=True)
        y = y * lax.rsqrt(ms + EPS) * fg_ref[...]
    o_ref[0] = y


def _mix_ffn(mixer_outs, z, gate_blk, gn, x, g1, w_out, sh, sc, g, g2, w_gu, w_down, final_g,
             *, even, final_norm, tm=512, ff_chunk=256):
    B, T, _ = x.shape
    tok = lambda b, i: (b, i, 0)
    per_b = lambda b, i: (b, 0, 0)
    const = lambda b, i: (0, 0)
    resident = functools.partial(pl.BlockSpec, index_map=const, pipeline_mode=pl.Buffered(1))
    nm = mixer_outs[0].shape[-1]
    in_specs = [pl.BlockSpec((1, tm, nm), tok) for _ in mixer_outs]
    in_specs += [
        pl.BlockSpec((1, tm, nm), lambda b, i: (b, i, gate_blk)),
        pl.BlockSpec((1, nm), const),
        pl.BlockSpec((1, tm, D_MODEL), tok),
        pl.BlockSpec((1, 1, D_MODEL), per_b),
        resident(w_out.shape),
        pl.BlockSpec((1, 1, D_MODEL), per_b),
        pl.BlockSpec((1, 1, D_MODEL), per_b),
        pl.BlockSpec((1, D_MODEL), const),
        pl.BlockSpec((1, 1, D_MODEL), per_b),
        resident(w_gu.shape),
        resident(w_down.shape),
        pl.BlockSpec((1, D_MODEL), const),
    ]
    kern = functools.partial(_mix_ffn_kernel, even=even, final_norm=final_norm, ff_chunk=ff_chunk)
    return pl.pallas_call(
        kern,
        grid=(B, T // tm),
        in_specs=in_specs,
        out_specs=pl.BlockSpec((1, tm, D_MODEL), tok),
        out_shape=jax.ShapeDtypeStruct((B, T, D_MODEL), F32),
        compiler_params=_params(("parallel", "parallel")),
        name=("mix_ffn_even" if even else "mix_ffn_odd") + ("_final" if final_norm else ""),
    )(*mixer_outs, z, gn, x, g1, w_out, sh, sc, g, g2, w_gu, w_down, final_g)


def _mlstm_kernel(gbr_ref, gbc_ref, qf_ref, kf_ref, vf_ref, gcf_ref, grf_ref, qb_ref, kb_ref, vb_ref, gcb_ref,
                  grb_ref, of_ref, ob_ref, c_scr, m_scr, *, tg):
    L = C_CHUNK
    nchunk = tg // L
    dk, dv = C_QK_DIM, C_V_DIM

    @pl.when(pl.program_id(1) == 0)
    def _():
        c_scr[...] = jnp.zeros(c_scr.shape, F32)
        m_scr[...] = jnp.zeros(m_scr.shape, F32)

    tt = lax.broadcasted_iota(jnp.int32, (L, L), 0)
    ss = lax.broadcasted_iota(jnp.int32, (L, L), 1)

    dirs = ((qf_ref, kf_ref, vf_ref, gcf_ref, grf_ref, of_ref),
            (qb_ref, kb_ref, vb_ref, gcb_ref, grb_ref, ob_ref))
    keeps = (ss <= tt, ss >= tt)
    tri_col = [keeps[0].astype(BF16), keeps[1].astype(BF16)]
    tri_row = [keeps[1].astype(BF16), keeps[0].astype(BF16)]
    glane = lax.broadcasted_iota(jnp.int32, (1, GATE_PAD), 1)
    f_lane = jnp.logical_and((glane & C_HEADS) != 0, glane < 4 * C_HEADS)
    f_row = (lax.broadcasted_iota(jnp.int32, (4 * C_HEADS, 1), 0) & C_HEADS) != 0
    csl = [slice(c * L, (c + 1) * L) for c in range(nchunk)]
    items = [(c, d) for c in range(nchunk) for d in range(2)]
    chains = [(c, d, h) for c, d in items for h in range(C_HEADS)]
    fi = {(d, h): (2 * d + 1) * C_HEADS + h for d in range(2) for h in range(C_HEADS)}

    Gc = []
    for d in range(2):
        g = dirs[d][3][0] + gbr_ref[...]
        Gc.append(jnp.where(f_lane, _log_sigmoid(g), g))
    Gr = {}
    for c, d in items:
        g = dirs[d][4][0, c] + gbc_ref[...]
        Gr[c, d] = jnp.where(f_row, _log_sigmoid(g), g)
    Gi = [pltpu.roll(Gc[d], C_HEADS, axis=1) for d in range(2)]
    Gri = {i: jnp.roll(Gr[i], C_HEADS, axis=0) for i in items}
    Fc = {(c, d): _dot_exact_lhs(tri_col[d], Gc[d][csl[c]]) for c, d in items}
    Fr = {(c, d): _dot_exact_rhs(Gr[c, d], tri_row[d]) for c, d in items}
    FLl = {(c, d): Fc[c, d][L - 1:L] if d == 0 else Fc[c, d][0:1] for c, d in items}
    WL = {(c, d): FLl[c, d] - Fc[c, d] + Gi[d][csl[c]] for c, d in items}
    wl_max = {i: jnp.max(WL[i], axis=0, keepdims=True) for i in items}
    w_r = {i: Gri[i] - Fr[i] for i in items}
    neg = jnp.full((L, GATE_PAD), -jnp.inf, F32)
    P = {}
    for c, d in items:
        x = Gi[d][csl[c]] - Fc[c, d]
        sh = 1
        while sh < L:
            shifted = (jnp.concatenate([neg[:sh], x[:L - sh]], axis=0) if d == 0
                       else jnp.concatenate([x[sh:], neg[:sh]], axis=0))
            x = jnp.maximum(x, shifted)
            sh *= 2
        P[c, d] = x
    m_old_l, m_new_l = {}, {}
    ml = [m_scr[d] for d in range(2)]
    for ci in range(nchunk):
        for d in range(2):
            c = ci if d == 0 else nchunk - 1 - ci
            m_old_l[c, d] = ml[d]
            ml[d] = jnp.maximum(FLl[c, d] + ml[d], wl_max[c, d])
            m_new_l[c, d] = ml[d]
    MT = {i: Fc[i] + jnp.maximum(m_old_l[i], P[i]) for i in items}
    U = {i: Fc[i] - MT[i] for i in items}
    INTER = {i: jnp.exp(Fc[i] + m_old_l[i] - MT[i]) for i in items}
    EMT = {i: jnp.exp(-MT[i]) for i in items}
    WS = {i: jnp.exp(WL[i] - m_new_l[i]) for i in items}
    DEC = {i: jnp.exp(FLl[i] + m_old_l[i] - m_new_l[i]) for i in items}
    na = dv + dk
    col = lambda X, k_, n: jnp.broadcast_to(X[k_[0], k_[1]][:, fi[k_[1], k_[2]]:fi[k_[1], k_[2]] + 1], (L, n))
    qc = {(c, d, h): dirs[d][0][0, csl[c], h * dk:(h + 1) * dk] for c, d, h in chains}
    kf = {(c, d, h): dirs[d][1][0, csl[c], h * dk:(h + 1) * dk].astype(F32) * (dk ** -0.5) for c, d, h in chains}
    ones = jnp.ones((L, dk), BF16)
    va = {(c, d, h): jnp.concatenate([dirs[d][2][0, csl[c], h * dv:(h + 1) * dv], ones], axis=1)
          for c, d, h in chains}
    qk = {k_: lax.dot_general(qc[k_], kf[k_].astype(BF16), _NT, preferred_element_type=F32) for k_ in chains}
    W = {k_: (jnp.exp(jnp.where(keeps[k_[1]], col(U, k_, L) + w_r[k_[0], k_[1]][fi[k_[1], k_[2]]:fi[k_[1], k_[2]] + 1],
                                -jnp.inf)) * qk[k_]).astype(BF16) for k_ in chains}
    Wv = {k_: jnp.dot(W[k_], va[k_], preferred_element_type=F32) for k_ in chains}
    kw = {k_: (kf[k_] * col(WS, k_, dk)).astype(BF16) for k_ in chains}
    kv = {k_: lax.dot_general(kw[k_], va[k_], _TN, preferred_element_type=F32) for k_ in chains}
    inter = {k_: col(INTER, k_, dk) for k_ in chains}
    e_mt = {k_: col(EMT, k_, dk) for k_ in chains}
    dec = {(c, d, h): DEC[c, d][:, fi[d, h]:fi[d, h] + 1] for c, d, h in chains}
    Cs = {(d, h): c_scr[d, h] for d in range(2) for h in range(C_HEADS)}
    for ci in range(nchunk):
        for d in range(2):
            c = ci if d == 0 else nchunk - 1 - ci
            for h in range(C_HEADS):
                k_ = (c, d, h)
                C = Cs[d, h]
                g = inter[k_]
                num = (jnp.concatenate([g] * (na // dk), axis=1)
                       * jnp.dot(qc[k_], C.astype(BF16), preferred_element_type=F32) + Wv[k_])
                den = jnp.maximum(jnp.abs(num[:, dv:]), e_mt[k_])
                dirs[d][5][0, csl[c], h * dv:(h + 1) * dv] = num[:, :dv] / jnp.concatenate([den] * (dv // dk), axis=1)
                Cs[d, h] = dec[k_] * C + kv[k_]
    for d in range(2):
        for h in range(C_HEADS):
            c_scr[d, h] = Cs[d, h]
        m_scr[d] = ml[d]


def _mlstm(z, gcol, grow, gate_b, *, tg=256):
    B, T, _ = z.shape
    nb = T // tg
    qw = C_HEADS * C_QK_DIM
    vw = C_HEADS * C_V_DIM
    ng = grow.shape[2]
    fwd = lambda b, i: (b, i)
    bwd = lambda b, i: (b, nb - 1 - i)

    def specs(order):
        return [
            pl.BlockSpec((1, tg, qw), lambda b, i: (*order(b, i), 0)),
            pl.BlockSpec((1, tg, qw), lambda b, i: (*order(b, i), 1)),
            pl.BlockSpec((1, tg, vw), lambda b, i: (*order(b, i), 1)),
            pl.BlockSpec((1, tg, GATE_PAD), lambda b, i: (*order(b, i), 0)),
            pl.BlockSpec((1, tg // C_CHUNK, ng, C_CHUNK), lambda b, i: (*order(b, i), 0, 0)),
        ]

    kern = functools.partial(_mlstm_kernel, tg=tg)
    return pl.pallas_call(
        kern,
        grid=(B, nb),
        in_specs=[pl.BlockSpec((1, GATE_PAD), lambda b, i: (0, 0)),
                  pl.BlockSpec((ng, 1), lambda b, i: (0, 0))] + specs(fwd) + specs(bwd),
        out_specs=[
            pl.BlockSpec((1, tg, vw), lambda b, i: (b, i, 0)),
            pl.BlockSpec((1, tg, vw), lambda b, i: (b, nb - 1 - i, 0)),
        ],
        out_shape=[jax.ShapeDtypeStruct((B, T, vw), F32)] * 2,
        scratch_shapes=[
            pltpu.VMEM((2, C_HEADS, C_QK_DIM, C_V_DIM + C_QK_DIM), F32),
            pltpu.VMEM((2, 1, GATE_PAD), F32),
        ],
        compiler_params=_params(("parallel", "arbitrary")),
        name="mlstm",
    )(_pad_cols(gate_b[None, :], GATE_PAD), gate_b[:, None], z, z, z, gcol, grow, z, z, z, gcol, grow)


def _pad_cols(w, n):
    return jnp.pad(w, ((0, 0), (0, n - w.shape[1])))


def _trunk(x, mod, p):
    for i in range(DEPTH):
        sh1, sc1, g1, sh2, sc2, g2 = [m[:, None, :] for m in jnp.split(mod[i], 6, axis=-1)]
        if i % 2 == 0:
            j = i // 2
            lam_init = 0.8 - 0.6 * math.exp(-0.3 * i)
            z, zlr, vt = _inproj(x, sh1, sc1, p["norm1_g"][i], p["even_w_main"][j], p["even_w_lr"][j],
                                 p["even_w_vt"][j], _alibi_key_features(), with_rows=False)
            a = _diff_attention(z, vt, p["even_lam"][j], p["even_sub_g"][j], lam_init)
            of, ob = _gla(z, zlr, p["even_gk_w"][j], p["even_gk_b"][j])
            mixer, gate_blk, gn, w_out = (a, of, ob), EVEN_BG // a.shape[-1], p["even_gla_norm_g"][j], p["even_w_out"][j]
        else:
            j = i // 2
            z, gcol, grow = _inproj(x, sh1, sc1, p["norm1_g"][i], p["odd_w_main"][j], p["odd_w_gate"][j],
                                    p["odd_w_gate_t"][j], with_rows=True, chunk=C_CHUNK)
            hf, hb = _mlstm(z, gcol, grow, p["odd_gate_b"][j])
            mixer, gate_blk, gn, w_out = (hf, hb), ODD_CO // hf.shape[-1], p["odd_norm_g"][j], p["odd_w_out"][j]
        x = _mix_ffn(mixer, z, gate_blk, gn, x, g1, w_out, sh2, sc2, p["norm2_g"][i], g2, p["ffn_w_gu"][i],
                     p["ffn_w_down"][i], p["final_g"], even=(i % 2 == 0), final_norm=(i == DEPTH - 1))
    return x


def kernel(x_prompt, x_sample, c_prompt, c_sample, w_mod, b_mod, norm1_g, norm2_g, even_w_in, even_lam_q1, even_lam_k1, even_lam_q2, even_lam_k2, even_attn_sub_g, even_gk_w_f, even_gk_b_f, even_gk_w_b, even_gk_b_b, even_gla_norm_g, even_w_out, odd_w_in, odd_gate_b, odd_norm_g, odd_w_out, ffn_w_gu, ffn_w_down, final_g):
    n_even = even_w_in.shape[0]
    n_odd = odd_w_in.shape[0]
    r = B_GATE_RANK
    kw = B_HEADS * B_K_DIM
    gk_w = jnp.zeros((n_even, 2, GATE_PAD, kw), F32)
    gk_w = gk_w.at[:, 0, 0:r].set(even_gk_w_f).at[:, 1, r:2 * r].set(even_gk_w_b)
    p = {
        "norm1_g": norm1_g[:, None, :],
        "norm2_g": norm2_g[:, None, :],
        "final_g": final_g[None, :],
        "even_w_main": jnp.concatenate([even_w_in[:, :, :EVEN_AV], even_w_in[:, :, EVEN_REST:EVEN_MAIN]],
                                       axis=-1).astype(BF16),
        "even_w_vt": jnp.swapaxes(even_w_in[:, :, EVEN_AV:EVEN_REST], 1, 2).astype(BF16),
        "even_w_lr": jnp.stack([_pad_cols(even_w_in[j, :, EVEN_MAIN:], GATE_PAD) for j in range(n_even)]),
        "even_lam": jnp.stack([even_lam_q1, even_lam_k1, even_lam_q2, even_lam_k2], axis=1),
        "even_sub_g": even_attn_sub_g[:, :, None],
        "even_gk_w": gk_w,
        "even_gk_b": jnp.stack([even_gk_b_f, even_gk_b_b], axis=1)[:, :, None, :],
        "even_gla_norm_g": even_gla_norm_g[:, None, :],
        "even_w_out": even_w_out.astype(BF16),
        "odd_w_main": odd_w_in[:, :, :ODD_MAIN].astype(BF16),
        "odd_w_gate": jnp.stack([_pad_cols(odd_w_in[j, :, ODD_MAIN:], GATE_PAD) for j in range(n_odd)]),
        "odd_w_gate_t": jnp.swapaxes(odd_w_in[:, :, ODD_MAIN:], 1, 2),
        "odd_gate_b": odd_gate_b,
        "odd_norm_g": odd_norm_g[:, None, :],
        "odd_w_out": odd_w_out.astype(BF16),
        "ffn_w_gu": ffn_w_gu.astype(BF16),
        "ffn_w_down": ffn_w_down.astype(BF16),
    }
    nbp = c_prompt.shape[0]
    mod = _modulation(jnp.concatenate([c_prompt, c_sample], axis=0), w_mod, b_mod)
    y_prompt = _trunk(x_prompt, mod[:, :nbp], p)
    y_sample = _trunk(x_sample, mod[:, nbp:], p)
    return (y_prompt, y_sample)
```

```python
import functools
import math

import jax
import jax.numpy as jnp
from jax import lax
from jax.experimental import pallas as pl
from jax.experimental.pallas import tpu as pltpu

F32 = jnp.float32
BF16 = jnp.bfloat16
HIGHEST = lax.Precision.HIGHEST

D_MODEL = 1024
DEPTH = 2
EPS = 1e-6
A_HEADS = 4
A_QK_DIM = 64
A_V_DIM = 128
B_HEADS = 4
B_K_DIM = 64
B_V_DIM = 128
B_GATE_RANK = 16
B_GATE_NORM = 16.0
B_CHUNK = 64
C_HEADS = 4
C_QK_DIM = 128
C_V_DIM = 256
C_CHUNK = 64
D_FF = 2816
EVEN_AV = 1024
EVEN_REST = 1536
EVEN_MAIN = 3072
EVEN_AK = 512
EVEN_BQ = 1536
EVEN_BV = 2048
EVEN_BG = 2560
ODD_MAIN = 3072
ODD_CO = 2048
GATE_PAD = 128
ONES_ROWS = 16
LOG2E = 1.4426950408889634
ATTN_SUB = 256
ATTN_AHEAD = 2
ATTN_STREAMS = 4

VMEM_LIMIT = 56 * 1024 * 1024

_NT = (((1,), (1,)), ((), ()))
_TN = (((0,), (0,)), ((), ()))


def _log_sigmoid(x):
    return jnp.minimum(x, 0.0) - jnp.log1p(jnp.exp(-jnp.abs(x)))


def _silu(x):
    return x * jax.nn.sigmoid(x)


def _split_bf16(x, terms):
    parts = []
    for _ in range(terms - 1):
        hi = x.astype(BF16)
        parts.append(hi)
        x = x - hi.astype(F32)
    parts.append(x.astype(BF16))
    return parts


def _dot_exact_lhs(a_bf16, x):
    return sum(jnp.dot(a_bf16, part, preferred_element_type=F32) for part in _split_bf16(x, 3))


def _dot_exact_rhs(x, a_bf16):
    return sum(jnp.dot(part, a_bf16, preferred_element_type=F32) for part in _split_bf16(x, 3))


def _dot_split2(x, w, dims=(((1,), (0,)), ((), ()))):
    xh, xl = _split_bf16(x, 2)
    wh, wl = _split_bf16(w, 2)
    dot = functools.partial(lax.dot_general, dimension_numbers=dims, preferred_element_type=F32)
    return dot(xh, wh) + dot(xh, wl) + dot(xl, wh)


def _params(sem, flags=None):
    return pltpu.CompilerParams(dimension_semantics=sem, vmem_limit_bytes=VMEM_LIMIT, flags=flags)


def _mod_kernel(c_ref, w_ref, b_ref, o_ref):
    s = _silu(c_ref[...])
    o_ref[0] = jnp.dot(s, w_ref[0], precision=HIGHEST, preferred_element_type=F32) + b_ref[0]


def _modulation(c_all, w_mod, b_mod):
    nb = c_all.shape[0]
    tn = 1536
    return pl.pallas_call(
        _mod_kernel,
        grid=(DEPTH, 6 * D_MODEL // tn),
        in_specs=[
            pl.BlockSpec((nb, D_MODEL), lambda i, j: (0, 0)),
            pl.BlockSpec((1, D_MODEL, tn), lambda i, j: (i, 0, j)),
            pl.BlockSpec((1, 1, tn), lambda i, j: (i, 0, j)),
        ],
        out_specs=pl.BlockSpec((1, nb, tn), lambda i, j: (i, 0, j)),
        out_shape=jax.ShapeDtypeStruct((DEPTH, nb, 6 * D_MODEL), F32),
        compiler_params=_params(("parallel", "parallel")),
        name="modulation",
    )(c_all, w_mod, b_mod.reshape(DEPTH, 1, 6 * D_MODEL))


def _norm_mod(x, g, sh, sc):
    ms = jnp.mean(x * x, axis=-1, keepdims=True)
    y = x * lax.rsqrt(ms + EPS) * g
    return y * (1.0 + sc) + sh


def _inproj_kernel(x_ref, sh_ref, sc_ref, g_ref, w_ref, wg_ref, wt_ref, *rest, n_main, n_col, chunk, with_rows):
    if with_rows:
        z_ref, zg_ref, zt_ref = rest
    else:
        kf_ref, z_ref, zg_ref, zt_ref = rest
    h = _norm_mod(x_ref[0], g_ref[...], sh_ref[0], sc_ref[0])
    hb = h.astype(BF16)
    tm = h.shape[0]
    for j in range(n_main // n_col):
        zc = jnp.dot(hb, w_ref[:, j * n_col:(j + 1) * n_col], preferred_element_type=F32)
        if with_rows or j == 0:
            z_ref[0, :, j * n_col:(j + 1) * n_col] = zc.astype(BF16)
        elif j == 1:
            hw = 2 * A_QK_DIM
            dup = jnp.concatenate([zc[:, (i // 2) * hw:(i // 2 + 1) * hw] for i in range(2 * A_HEADS)], axis=1)
            pos = (lax.broadcasted_iota(jnp.int32, (tm, 1), 0) & (ATTN_SUB - 1)).astype(F32)
            feat = kf_ref[1:2, :] + kf_ref[2:3, :] * pos
            z_ref[0, :, n_col:3 * n_col] = jnp.where(kf_ref[0:1, :] > 0.0, dup, feat).astype(BF16)
        else:
            z_ref[0, :, (j + 1) * n_col:(j + 2) * n_col] = zc.astype(BF16)
    zg_ref[0] = _dot_split2(h, wg_ref[...])
    if with_rows:
        for c in range(tm // chunk):
            zt_ref[0, c] = _dot_split2(wt_ref[...], h[c * chunk:(c + 1) * chunk], _NT)
    else:
        for c in range(h.shape[0] // ATTN_SUB):
            zt_ref[0, c] = lax.dot_general(wt_ref[...], hb[c * ATTN_SUB:(c + 1) * ATTN_SUB], _NT,
                                           preferred_element_type=F32).astype(BF16)


def _inproj(x, sh, sc, g, w_main, w_gate, w_t, k_feat=None, *, with_rows, tm=512, chunk=64):
    B, T, _ = x.shape
    n_main = w_main.shape[1]
    n_col = 512
    n_z = n_main if with_rows else n_main + n_col
    kern = functools.partial(_inproj_kernel, n_main=n_main, n_col=n_col, chunk=chunk, with_rows=with_rows)
    in_specs = [
        pl.BlockSpec((1, tm, D_MODEL), lambda b, i: (b, i, 0)),
        pl.BlockSpec((1, 1, D_MODEL), lambda b, i: (b, 0, 0)),
        pl.BlockSpec((1, 1, D_MODEL), lambda b, i: (b, 0, 0)),
        pl.BlockSpec((1, D_MODEL), lambda b, i: (0, 0)),
        pl.BlockSpec((D_MODEL, n_main), lambda b, i: (0, 0)),
        pl.BlockSpec((D_MODEL, GATE_PAD), lambda b, i: (0, 0)),
    ]
    out_specs = [
        pl.BlockSpec((1, tm, n_z), lambda b, i: (b, i, 0)),
        pl.BlockSpec((1, tm, GATE_PAD), lambda b, i: (b, i, 0)),
    ]
    out_shape = [
        jax.ShapeDtypeStruct((B, T, n_z), BF16),
        jax.ShapeDtypeStruct((B, T, GATE_PAD), F32),
    ]
    args = [x, sh, sc, g, w_main, w_gate, w_t]
    nt = w_t.shape[0]
    in_specs.append(pl.BlockSpec((nt, D_MODEL), lambda b, i: (0, 0)))
    if with_rows:
        out_specs.append(pl.BlockSpec((1, tm // chunk, nt, chunk), lambda b, i: (b, i, 0, 0)))
        out_shape.append(jax.ShapeDtypeStruct((B, T // chunk, nt, chunk), F32))
    else:
        args.append(k_feat)
        in_specs.append(pl.BlockSpec(k_feat.shape, lambda b, i: (0, 0)))
        out_specs.append(pl.BlockSpec((1, tm // ATTN_SUB, nt, ATTN_SUB), lambda b, i: (b, i, 0, 0)))
        out_shape.append(jax.ShapeDtypeStruct((B, T // ATTN_SUB, nt, ATTN_SUB), BF16))
    return pl.pallas_call(
        kern,
        grid=(B, T // tm),
        in_specs=in_specs,
        out_specs=out_specs,
        out_shape=out_shape,
        compiler_params=_params(("parallel", "parallel")),
        name="inproj_rows" if with_rows else "inproj",
    )(*args)


def _attn_kernel(slope_ref, q_ref, ka_ref, kb_ref, vt_ref, lamp_ref, subg_ref, o_ref, q_scr, *, n_sub, lam_init):
    h = pl.program_id(1)
    tq = sk = ATTN_SUB
    dv = A_V_DIM
    hw = 2 * A_QK_DIM
    slope2 = slope_ref[h, 0] + slope_ref[h, 1] + slope_ref[h, 2]
    streams = range(ATTN_STREAMS)
    qis = [pl.program_id(2) * ATTN_STREAMS + st for st in streams]

    lane = lax.broadcasted_iota(jnp.int32, (1, hw), 1)
    c_off = lax.broadcasted_iota(jnp.int32, (tq, hw), 0).astype(F32)
    for c in range(2):
        own = (lane < A_QK_DIM) if c == 0 else (lane >= A_QK_DIM)
        base = (1 - c) * A_QK_DIM
        q_feat = jnp.zeros((tq, hw), F32)
        for x in range(3):
            q_feat = jnp.where(lane == base + x, slope_ref[h, x], q_feat)
            q_feat = jnp.where(lane == base + 3 + x, -c_off, q_feat)
        for st in streams:
            qs = (q_ref[0, st * tq:(st + 1) * tq, :].astype(F32) * (A_QK_DIM ** -0.5 * LOG2E)).astype(BF16)
            for v, sign in enumerate((1.0, -1.0, 0.0)):
                q_scr[st, v, c] = jnp.where(own, qs, (sign * q_feat).astype(BF16))
    rel = (lax.broadcasted_iota(jnp.int32, (sk, tq), 0)
           - lax.broadcasted_iota(jnp.int32, (sk, tq), 1)).astype(F32)
    diag_bias = -slope2 * jnp.abs(rel)
    ones = jnp.ones((ONES_ROWS, sk), BF16)
    k_refs = (ka_ref, kb_ref)

    def scores(st, j):
        qi = qis[st]
        if j == 0:
            idx, const, v = qi, jnp.zeros((), F32), 2
        else:
            idx = qi + j
            idx = jnp.where(idx >= n_sub, idx - n_sub, idx)
            before = idx < qi
            const = jnp.where(before, 1.0, -1.0) * slope2 * ((idx - qi) * sk).astype(F32)
            v = jnp.where(before, 0, 1)
        rows = pl.ds(pl.multiple_of(idx * sk, sk), sk)
        s = [lax.dot_general(k_refs[c][0, rows, :], q_scr[st, v, c], _NT, preferred_element_type=F32)
             for c in range(2)]
        if j == 0:
            s = [s_c + diag_bias for s_c in s]
        s = jnp.concatenate(s, axis=1)
        return s, jnp.max(s, axis=0, keepdims=True) + const, const, idx

    def accumulate(acc, alpha, p, idx):
        vt1 = jnp.concatenate([vt_ref[0, idx], ones], axis=0)
        return alpha * acc + jnp.dot(vt1, p, preferred_element_type=F32)

    m = [jnp.full((1, 2 * tq), -jnp.inf, F32) for _ in streams]
    acc = [jnp.zeros((dv + ONES_ROWS, 2 * tq), F32) for _ in streams]
    ahead = [[scores(st, j) for j in range(min(ATTN_AHEAD, n_sub))] for st in streams]
    pending = [None for _ in streams]
    for j in range(n_sub):
        cur = [ahead[st].pop(0) for st in streams]
        if j + ATTN_AHEAD < n_sub:
            for st in streams:
                ahead[st].append(scores(st, j + ATTN_AHEAD))
        for st in streams:
            s, m_loc, const, idx = cur[st]
            m_new = jnp.maximum(m[st], m_loc)
            alpha = jnp.exp2(m[st] - m_new)
            p = jnp.exp2(s - (m_new - const)).astype(BF16)
            m[st] = m_new
            cur[st] = (alpha, p, idx)
        for st in streams:
            if pending[st] is not None:
                acc[st] = accumulate(acc[st], *pending[st])
            pending[st] = cur[st]
    lp = lamp_ref[...]
    lam = (jnp.exp(jnp.sum(lp[0:1] * lp[1:2], axis=-1, keepdims=True))
           - jnp.exp(jnp.sum(lp[2:3] * lp[3:4], axis=-1, keepdims=True)) + lam_init)
    for st in streams:
        a = accumulate(acc[st], *pending[st])
        on = a[:dv] / a[dv:dv + 1]
        o = on[:, :tq] - lam * on[:, tq:]
        ms = jnp.mean(o * o, axis=0, keepdims=True)
        y = o * lax.rsqrt(ms + EPS) * subg_ref[...] * (1.0 - lam_init)
        o_ref[0, st * tq:(st + 1) * tq, :] = y.T.astype(o_ref.dtype)


def _alibi_slope_terms():
    slope2 = jnp.exp2(-8.0 * jnp.arange(1, A_HEADS + 1, dtype=F32) / A_HEADS) * LOG2E
    s0 = slope2.astype(BF16).astype(F32)
    s1 = (slope2 - s0).astype(BF16).astype(F32)
    s2 = (slope2 - s0 - s1).astype(BF16).astype(F32)
    return jnp.stack([s0, s1, s2], axis=1)


def _alibi_key_features():
    hw = 2 * A_QK_DIM
    terms = _alibi_slope_terms()
    lane = jnp.arange(hw)
    rows = []
    for h in range(A_HEADS):
        for c in range(2):
            base = (1 - c) * A_QK_DIM
            keep = ((lane < A_QK_DIM) if c == 0 else (lane >= A_QK_DIM)).astype(F32)
            const = jnp.zeros((hw,), F32).at[base + 3:base + 6].set(terms[h])
            pos = jnp.zeros((hw,), F32).at[base:base + 3].set(1.0)
            rows.append(jnp.stack([keep, const, pos]))
    return jnp.concatenate(rows, axis=1)


def _diff_attention(z, vt, lam_params, sub_g_col, lam_init):
    B, T, _ = z.shape
    n_sub = T // ATTN_SUB
    slopes = _alibi_slope_terms()
    kern = functools.partial(_attn_kernel, n_sub=n_sub, lam_init=lam_init)
    hw = 2 * A_QK_DIM
    return pl.pallas_call(
        kern,
        grid=(B, A_HEADS, n_sub // ATTN_STREAMS),
        in_specs=[
            pl.BlockSpec(memory_space=pltpu.SMEM),
            pl.BlockSpec((1, ATTN_STREAMS * ATTN_SUB, hw), lambda b, h, i: (b, i, h)),
            pl.BlockSpec((1, T, hw), lambda b, h, i: (b, 0, EVEN_AK // hw + 2 * h)),
            pl.BlockSpec((1, T, hw), lambda b, h, i: (b, 0, EVEN_AK // hw + 2 * h + 1)),
            pl.BlockSpec((1, n_sub, A_V_DIM, ATTN_SUB), lambda b, h, i: (b, 0, h, 0)),
            pl.BlockSpec((4, A_QK_DIM), lambda b, h, i: (0, 0)),
            pl.BlockSpec((A_V_DIM, 1), lambda b, h, i: (0, 0)),
        ],
        out_specs=pl.BlockSpec((1, ATTN_STREAMS * ATTN_SUB, A_V_DIM), lambda b, h, i: (b, i, h)),
        out_shape=jax.ShapeDtypeStruct((B, T, A_HEADS * A_V_DIM), BF16),
        scratch_shapes=[pltpu.VMEM((ATTN_STREAMS, 3, 2, ATTN_SUB, hw), BF16)],
        compiler_params=_params(("parallel", "parallel", "parallel")),
        name="diff_attention",
    )(slopes, z, z, z, vt, lam_params, sub_g_col)


def _gla_kernel(qf_ref, kf_ref, vf_ref, lrf_ref, qb_ref, kb_ref, vb_ref, lrb_ref, gw_ref, gb_ref,
                of_ref, ob_ref, s_scr, *, tg):
    L = B_CHUNK
    nchunk = tg // L
    pw = 2 * B_K_DIM
    vw = 2 * B_V_DIM

    @pl.when(pl.program_id(1) == 0)
    def _():
        s_scr[...] = jnp.zeros(s_scr.shape, F32)

    tt = lax.broadcasted_iota(jnp.int32, (L, L), 0)
    ss = lax.broadcasted_iota(jnp.int32, (L, L), 1)
    lane = lax.broadcasted_iota(jnp.int32, (1, pw), 1)
    srow = lax.broadcasted_iota(jnp.int32, (vw, pw), 0)
    slane = lax.broadcasted_iota(jnp.int32, (vw, pw), 1)
    same_head = (srow >= B_V_DIM) == (slane >= B_K_DIM)
    zero_b = jnp.zeros((), BF16)

    dirs = ((qf_ref, kf_ref, vf_ref, lrf_ref, of_ref), (qb_ref, kb_ref, vb_ref, lrb_ref, ob_ref))
    keeps = (ss <= tt, ss >= tt)
    tris = [kp.astype(BF16) for kp in keeps]
    lgs = []
    for d in range(2):
        pre = _dot_split2(dirs[d][3][0], gw_ref[d]) + gb_ref[d]
        lgs.append(_log_sigmoid(pre) * (1.0 / B_GATE_NORM))
    np_ = B_HEADS // 2
    states = [[s_scr[d, p] for p in range(np_)] for d in range(2)]
    sels = (lane < B_K_DIM, lane >= B_K_DIM)
    csl = [slice(c * L, (c + 1) * L) for c in range(nchunk)]
    ksl = [slice(p * pw, (p + 1) * pw) for p in range(np_)]
    items = [(c, d) for c in range(nchunk) for d in range(2)]
    chains = [(c, d, p) for c, d in items for p in range(np_)]
    bs = {(c, d): _dot_exact_lhs(tris[d], lgs[d][csl[c]]) for c, d in items}
    b_ends = {(c, d): bs[c, d][L - 1:L] if d == 0 else bs[c, d][0:1] for c, d in items}
    qcs = {(c, d): dirs[d][0][0, csl[c], :].astype(F32) for c, d in items}
    kcs = {(c, d): dirs[d][1][0, csl[c], :].astype(F32) for c, d in items}
    qes = {i: (qcs[i] * (B_K_DIM ** -0.5) * jnp.exp(bs[i])).astype(BF16) for i in items}
    kes = {i: (kcs[i] * jnp.exp(-bs[i])).astype(BF16) for i in items}
    kds = {i: (kcs[i] * jnp.exp(b_ends[i] - bs[i])).astype(BF16) for i in items}
    decs = {i: jnp.exp(b_ends[i]) for i in items}
    v_ps = {(c, d, p): dirs[d][2][0, csl[c], p * vw:(p + 1) * vw] for c, d, p in chains}
    atts = {(c, d, p, hh): lax.dot_general(jnp.where(sels[hh], qes[c, d][:, ksl[p]], zero_b), kes[c, d][:, ksl[p]],
                                           _NT, preferred_element_type=F32)
            for c, d, p in chains for hh in range(2)}
    atts = {key: jnp.where(keeps[key[1]], a, 0.0).astype(BF16) for key, a in atts.items()}
    intras = {(c, d, p): jnp.concatenate(
        [jnp.dot(atts[c, d, p, hh], v_ps[c, d, p][:, hh * B_V_DIM:(hh + 1) * B_V_DIM],
                 preferred_element_type=F32) for hh in range(2)], axis=-1) for c, d, p in chains}
    uts = {(c, d, p): jnp.where(same_head, lax.dot_general(v_ps[c, d, p], kds[c, d][:, ksl[p]], _TN,
                                                           preferred_element_type=F32), 0.0)
           for c, d, p in chains}
    for ci in range(nchunk):
        for d in range(2):
            c = ci if d == 0 else nchunk - 1 - ci
            for p in range(np_):
                st = states[d][p]
                inter = lax.dot_general(qes[c, d][:, ksl[p]], st.astype(BF16), _NT, preferred_element_type=F32)
                dirs[d][4][0, csl[c], p * vw:(p + 1) * vw] = inter + intras[c, d, p]
                states[d][p] = decs[c, d][:, ksl[p]] * st + uts[c, d, p]
    for d in range(2):
        for p in range(np_):
            s_scr[d, p] = states[d][p]


def _gla(z, zlr, gw, gb, *, tg=1024):
    B, T, _ = z.shape
    nb = T // tg
    kw = B_HEADS * B_K_DIM
    vw = B_HEADS * B_V_DIM
    q_blk = EVEN_BQ // kw
    v_blk = EVEN_BV // vw
    fwd = lambda b, i: (b, i)
    bwd = lambda b, i: (b, nb - 1 - i)

    def specs(order):
        return [
            pl.BlockSpec((1, tg, kw), lambda b, i: (*order(b, i), q_blk)),
            pl.BlockSpec((1, tg, kw), lambda b, i: (*order(b, i), q_blk + 1)),
            pl.BlockSpec((1, tg, vw), lambda b, i: (*order(b, i), v_blk)),
            pl.BlockSpec((1, tg, GATE_PAD), lambda b, i: (*order(b, i), 0)),
        ]

    kern = functools.partial(_gla_kernel, tg=tg)
    return pl.pallas_call(
        kern,
        grid=(B, nb),
        in_specs=specs(fwd) + specs(bwd) + [
            pl.BlockSpec((2, GATE_PAD, kw), lambda b, i: (0, 0, 0)),
            pl.BlockSpec((2, 1, kw), lambda b, i: (0, 0, 0)),
        ],
        out_specs=[
            pl.BlockSpec((1, tg, vw), lambda b, i: (b, i, 0)),
            pl.BlockSpec((1, tg, vw), lambda b, i: (b, nb - 1 - i, 0)),
        ],
        out_shape=[jax.ShapeDtypeStruct((B, T, vw), F32)] * 2,
        scratch_shapes=[pltpu.VMEM((2, B_HEADS // 2, 2 * B_V_DIM, 2 * B_K_DIM), F32)],
        compiler_params=_params(("parallel", "arbitrary")),
        name="gla",
    )(z, z, z, zlr, z, z, z, zlr, gw, gb)


def _head_rmsnorm(o, g, width):
    parts = []
    for h in range(o.shape[-1] // width):
        oh = o[:, h * width:(h + 1) * width]
        ms = jnp.mean(oh * oh, axis=-1, keepdims=True)
        parts.append(oh * lax.rsqrt(ms + EPS) * g[:, h * width:(h + 1) * width])
    return jnp.concatenate(parts, axis=-1)


def _mix_ffn_kernel(*refs, even, final_norm, ff_chunk):
    if even:
        a_ref, of_ref, ob_ref, gate_ref = refs[:4]
        refs = refs[4:]
    else:
        hf_ref, hb_ref, gate_ref = refs[:3]
        refs = refs[3:]
    gn_ref, x_ref, g1_ref, wo_ref, sh_ref, sc_ref, g_ref, g2_ref, wgu_ref, wd_ref, fg_ref, o_ref = refs
    if even:
        o = _head_rmsnorm(of_ref[0] + ob_ref[0], gn_ref[...], B_V_DIM)
        b_out = (o * _silu(gate_ref[0].astype(F32))).astype(BF16)
        mix = jnp.concatenate([a_ref[0], b_out], axis=-1)
    else:
        ht = _head_rmsnorm(hf_ref[0] + hb_ref[0], gn_ref[...], C_V_DIM)
        mix = (jax.nn.sigmoid(gate_ref[0].astype(F32)) * ht).astype(BF16)
    x1 = x_ref[0] + g1_ref[0] * jnp.dot(mix, wo_ref[...], preferred_element_type=F32)
    hb = _norm_mod(x1, g_ref[...], sh_ref[0], sc_ref[0]).astype(BF16)
    acts = []
    for c in range(D_FF // ff_chunk):
        gate = jnp.dot(hb, wgu_ref[:, c * ff_chunk:(c + 1) * ff_chunk], preferred_element_type=F32)
        up = jnp.dot(hb, wgu_ref[:, D_FF + c * ff_chunk:D_FF + (c + 1) * ff_chunk], preferred_element_type=F32)
        acts.append((_silu(gate) * up).astype(BF16))
    act = jnp.concatenate(acts, axis=-1)
    y = x1 + g2_ref[0] * jnp.dot(act, wd_ref[...], preferred_element_type=F32)
    if final_norm:
        ms = jnp.mean(y * y, axis=-1, keepdims=True)
        y = y * lax.rsqrt(ms + EPS) * fg_ref[...]
    o_ref[0] = y


def _mix_ffn(mixer_outs, z, gate_blk, gn, x, g1, w_out, sh, sc, g, g2, w_gu, w_down, final_g,
             *, even, final_norm, tm=512, ff_chunk=256):
    B, T, _ = x.shape
    tok = lambda b, i: (b, i, 0)
    per_b = lambda b, i: (b, 0, 0)
    const = lambda b, i: (0, 0)
    resident = functools.partial(pl.BlockSpec, index_map=const, pipeline_mode=pl.Buffered(1))
    nm = mixer_outs[0].shape[-1]
    in_specs = [pl.BlockSpec((1, tm, nm), tok) for _ in mixer_outs]
    in_specs += [
        pl.BlockSpec((1, tm, nm), lambda b, i: (b, i, gate_blk)),
        pl.BlockSpec((1, nm), const),
        pl.BlockSpec((1, tm, D_MODEL), tok),
        pl.BlockSpec((1, 1, D_MODEL), per_b),
        resident(w_out.shape),
        pl.BlockSpec((1, 1, D_MODEL), per_b),
        pl.BlockSpec((1, 1, D_MODEL), per_b),
        pl.BlockSpec((1, D_MODEL), const),
        pl.BlockSpec((1, 1, D_MODEL), per_b),
        resident(w_gu.shape),
        resident(w_down.shape),
        pl.BlockSpec((1, D_MODEL), const),
    ]
    kern = functools.partial(_mix_ffn_kernel, even=even, final_norm=final_norm, ff_chunk=ff_chunk)
    return pl.pallas_call(
        kern,
        grid=(B, T // tm),
        in_specs=in_specs,
        out_specs=pl.BlockSpec((1, tm, D_MODEL), tok),
        out_shape=jax.ShapeDtypeStruct((B, T, D_MODEL), F32),
        compiler_params=_params(("parallel", "parallel")),
        name=("mix_ffn_even" if even else "mix_ffn_odd") + ("_final" if final_norm else ""),
    )(*mixer_outs, z, gn, x, g1, w_out, sh, sc, g, g2, w_gu, w_down, final_g)


def _mlstm_kernel(gbr_ref, gbc_ref, qf_ref, kf_ref, vf_ref, gcf_ref, grf_ref, qb_ref, kb_ref, vb_ref, gcb_ref,
                  grb_ref, of_ref, ob_ref, c_scr, m_scr, *, tg):
    L = C_CHUNK
    nchunk = tg // L
    dk, dv = C_QK_DIM, C_V_DIM

    @pl.when(pl.program_id(1) == 0)
    def _():
        c_scr[...] = jnp.zeros(c_scr.shape, F32)
        m_scr[...] = jnp.zeros(m_scr.shape, F32)

    tt = lax.broadcasted_iota(jnp.int32, (L, L), 0)
    ss = lax.broadcasted_iota(jnp.int32, (L, L), 1)

    dirs = ((qf_ref, kf_ref, vf_ref, gcf_ref, grf_ref, of_ref),
            (qb_ref, kb_ref, vb_ref, gcb_ref, grb_ref, ob_ref))
    keeps = (ss <= tt, ss >= tt)
    tri_col = [keeps[0].astype(BF16), keeps[1].astype(BF16)]
    tri_row = [keeps[1].astype(BF16), keeps[0].astype(BF16)]
    glane = lax.broadcasted_iota(jnp.int32, (1, GATE_PAD), 1)
    f_lane = jnp.logical_and((glane & C_HEADS) != 0, glane < 4 * C_HEADS)
    f_row = (lax.broadcasted_iota(jnp.int32, (4 * C_HEADS, 1), 0) & C_HEADS) != 0
    csl = [slice(c * L, (c + 1) * L) for c in range(nchunk)]
    items = [(c, d) for c in range(nchunk) for d in range(2)]
    chains = [(c, d, h) for c, d in items for h in range(C_HEADS)]
    fi = {(d, h): (2 * d + 1) * C_HEADS + h for d in range(2) for h in range(C_HEADS)}

    Gc = []
    for d in range(2):
        g = dirs[d][3][0] + gbr_ref[...]
        Gc.append(jnp.where(f_lane, _log_sigmoid(g), g))
    Gr = {}
    for c, d in items:
        g = dirs[d][4][0, c] + gbc_ref[...]
        Gr[c, d] = jnp.where(f_row, _log_sigmoid(g), g)
    Gi = [pltpu.roll(Gc[d], C_HEADS, axis=1) for d in range(2)]
    Gri = {i: jnp.roll(Gr[i], C_HEADS, axis=0) for i in items}
    Fc = {(c, d): _dot_exact_lhs(tri_col[d], Gc[d][csl[c]]) for c, d in items}
    Fr = {(c, d): _dot_exact_rhs(Gr[c, d], tri_row[d]) for c, d in items}
    FLl = {(c, d): Fc[c, d][L - 1:L] if d == 0 else Fc[c, d][0:1] for c, d in items}
    WL = {(c, d): FLl[c, d] - Fc[c, d] + Gi[d][csl[c]] for c, d in items}
    wl_max = {i: jnp.max(WL[i], axis=0, keepdims=True) for i in items}
    w_r = {i: Gri[i] - Fr[i] for i in items}
    neg = jnp.full((L, GATE_PAD), -jnp.inf, F32)
    P = {}
    for c, d in items:
        x = Gi[d][csl[c]] - Fc[c, d]
        sh = 1
        while sh < L:
            shifted = (jnp.concatenate([neg[:sh], x[:L - sh]], axis=0) if d == 0
                       else jnp.concatenate([x[sh:], neg[:sh]], axis=0))
            x = jnp.maximum(x, shifted)
            sh *= 2
        P[c, d] = x
    m_old_l, m_new_l = {}, {}
    ml = [m_scr[d] for d in range(2)]
    for ci in range(nchunk):
        for d in range(2):
            c = ci if d == 0 else nchunk - 1 - ci
            m_old_l[c, d] = ml[d]
            ml[d] = jnp.maximum(FLl[c, d] + ml[d], wl_max[c, d])
            m_new_l[c, d] = ml[d]
    MT = {i: Fc[i] + jnp.maximum(m_old_l[i], P[i]) for i in items}
    U = {i: Fc[i] - MT[i] for i in items}
    INTER = {i: jnp.exp(Fc[i] + m_old_l[i] - MT[i]) for i in items}
    EMT = {i: jnp.exp(-MT[i]) for i in items}
    WS = {i: jnp.exp(WL[i] - m_new_l[i]) for i in items}
    DEC = {i: jnp.exp(FLl[i] + m_old_l[i] - m_new_l[i]) for i in items}
    na = dv + dk
    col = lambda X, k_, n: jnp.broadcast_to(X[k_[0], k_[1]][:, fi[k_[1], k_[2]]:fi[k_[1], k_[2]] + 1], (L, n))
    qc = {(c, d, h): dirs[d][0][0, csl[c], h * dk:(h + 1) * dk] for c, d, h in chains}
    kf = {(c, d, h): dirs[d][1][0, csl[c], h * dk:(h + 1) * dk].astype(F32) * (dk ** -0.5) for c, d, h in chains}
    ones = jnp.ones((L, dk), BF16)
    va = {(c, d, h): jnp.concatenate([dirs[d][2][0, csl[c], h * dv:(h + 1) * dv], ones], axis=1)
          for c, d, h in chains}
    qk = {k_: lax.dot_general(qc[k_], kf[k_].astype(BF16), _NT, preferred_element_type=F32) for k_ in chains}
    W = {k_: (jnp.exp(jnp.where(keeps[k_[1]], col(U, k_, L) + w_r[k_[0], k_[1]][fi[k_[1], k_[2]]:fi[k_[1], k_[2]] + 1],
                                -jnp.inf)) * qk[k_]).astype(BF16) for k_ in chains}
    Wv = {k_: jnp.dot(W[k_], va[k_], preferred_element_type=F32) for k_ in chains}
    kw = {k_: (kf[k_] * col(WS, k_, dk)).astype(BF16) for k_ in chains}
    kv = {k_: lax.dot_general(kw[k_], va[k_], _TN, preferred_element_type=F32) for k_ in chains}
    inter = {k_: col(INTER, k_, dk) for k_ in chains}
    e_mt = {k_: col(EMT, k_, dk) for k_ in chains}
    dec = {(c, d, h): DEC[c, d][:, fi[d, h]:fi[d, h] + 1] for c, d, h in chains}
    Cs = {(d, h): c_scr[d, h] for d in range(2) for h in range(C_HEADS)}
    for ci in range(nchunk):
        for d in range(2):
            c = ci if d == 0 else nchunk - 1 - ci
            for h in range(C_HEADS):
                k_ = (c, d, h)
                C = Cs[d, h]
                g = inter[k_]
                num = (jnp.concatenate([g] * (na // dk), axis=1)
                       * jnp.dot(qc[k_], C.astype(BF16), preferred_element_type=F32) + Wv[k_])
                den = jnp.maximum(jnp.abs(num[:, dv:]), e_mt[k_])
                dirs[d][5][0, csl[c], h * dv:(h + 1) * dv] = num[:, :dv] / jnp.concatenate([den] * (dv // dk), axis=1)
                Cs[d, h] = dec[k_] * C + kv[k_]
    for d in range(2):
        for h in range(C_HEADS):
            c_scr[d, h] = Cs[d, h]
        m_scr[d] = ml[d]


def _mlstm(z, gcol, grow, gate_b, *, tg=512):
    B, T, _ = z.shape
    nb = T // tg
    qw = C_HEADS * C_QK_DIM
    vw = C_HEADS * C_V_DIM
    ng = grow.shape[2]
    fwd = lambda b, i: (b, i)
    bwd = lambda b, i: (b, nb - 1 - i)

    def specs(order):
        return [
            pl.BlockSpec((1, tg, qw), lambda b, i: (*order(b, i), 0)),
            pl.BlockSpec((1, tg, qw), lambda b, i: (*order(b, i), 1)),
            pl.BlockSpec((1, tg, vw), lambda b, i: (*order(b, i), 1)),
            pl.BlockSpec((1, tg, GATE_PAD), lambda b, i: (*order(b, i), 0)),
            pl.BlockSpec((1, tg // C_CHUNK, ng, C_CHUNK), lambda b, i: (*order(b, i), 0, 0)),
        ]

    kern = functools.partial(_mlstm_kernel, tg=tg)
    return pl.pallas_call(
        kern,
        grid=(B, nb),
        in_specs=[pl.BlockSpec((1, GATE_PAD), lambda b, i: (0, 0)),
                  pl.BlockSpec((ng, 1), lambda b, i: (0, 0))] + specs(fwd) + specs(bwd),
        out_specs=[
            pl.BlockSpec((1, tg, vw), lambda b, i: (b, i, 0)),
            pl.BlockSpec((1, tg, vw), lambda b, i: (b, nb - 1 - i, 0)),
        ],
        out_shape=[jax.ShapeDtypeStruct((B, T, vw), F32)] * 2,
        scratch_shapes=[
            pltpu.VMEM((2, C_HEADS, C_QK_DIM, C_V_DIM + C_QK_DIM), F32),
            pltpu.VMEM((2, 1, GATE_PAD), F32),
        ],
        compiler_params=_params(("parallel", "arbitrary")),
        name="mlstm",
    )(_pad_cols(gate_b[None, :], GATE_PAD), gate_b[:, None], z, z, z, gcol, grow, z, z, z, gcol, grow)


def _pad_cols(w, n):
    return jnp.pad(w, ((0, 0), (0, n - w.shape[1])))


def _trunk(x, mod, p):
    for i in range(DEPTH):
        sh1, sc1, g1, sh2, sc2, g2 = [m[:, None, :] for m in jnp.split(mod[i], 6, axis=-1)]
        if i % 2 == 0:
            j = i // 2
            lam_init = 0.8 - 0.6 * math.exp(-0.3 * i)
            z, zlr, vt = _inproj(x, sh1, sc1, p["norm1_g"][i], p["even_w_main"][j], p["even_w_lr"][j],
                                 p["even_w_vt"][j], _alibi_key_features(), with_rows=False)
            a = _diff_attention(z, vt, p["even_lam"][j], p["even_sub_g"][j], lam_init)
            of, ob = _gla(z, zlr, p["even_gk_w"][j], p["even_gk_b"][j])
            mixer, gate_blk, gn, w_out = (a, of, ob), EVEN_BG // a.shape[-1], p["even_gla_norm_g"][j], p["even_w_out"][j]
        else:
            j = i // 2
            z, gcol, grow = _inproj(x, sh1, sc1, p["norm1_g"][i], p["odd_w_main"][j], p["odd_w_gate"][j],
                                    p["odd_w_gate_t"][j], with_rows=True, chunk=C_CHUNK)
            hf, hb = _mlstm(z, gcol, grow, p["odd_gate_b"][j])
            mixer, gate_blk, gn, w_out = (hf, hb), ODD_CO // hf.shape[-1], p["odd_norm_g"][j], p["odd_w_out"][j]
        x = _mix_ffn(mixer, z, gate_blk, gn, x, g1, w_out, sh2, sc2, p["norm2_g"][i], g2, p["ffn_w_gu"][i],
                     p["ffn_w_down"][i], p["final_g"], even=(i % 2 == 0), final_norm=(i == DEPTH - 1))
    return x


def kernel(x_prompt, x_sample, c_prompt, c_sample, w_mod, b_mod, norm1_g, norm2_g, even_w_in, even_lam_q1, even_lam_k1, even_lam_q2, even_lam_k2, even_attn_sub_g, even_gk_w_f, even_gk_b_f, even_gk_w_b, even_gk_b_b, even_gla_norm_g, even_w_out, odd_w_in, odd_gate_b, odd_norm_g, odd_w_out, ffn_w_gu, ffn_w_down, final_g):
    n_even = even_w_in.shape[0]
    n_odd = odd_w_in.shape[0]
    r = B_GATE_RANK
    kw = B_HEADS * B_K_DIM
    gk_w = jnp.zeros((n_even, 2, GATE_PAD, kw), F32)
    gk_w = gk_w.at[:, 0, 0:r].set(even_gk_w_f).at[:, 1, r:2 * r].set(even_gk_w_b)
    p = {
        "norm1_g": norm1_g[:, None, :],
        "norm2_g": norm2_g[:, None, :],
        "final_g": final_g[None, :],
        "even_w_main": jnp.concatenate([even_w_in[:, :, :EVEN_AV], even_w_in[:, :, EVEN_REST:EVEN_MAIN]],
                                       axis=-1).astype(BF16),
        "even_w_vt": jnp.swapaxes(even_w_in[:, :, EVEN_AV:EVEN_REST], 1, 2).astype(BF16),
        "even_w_lr": jnp.stack([_pad_cols(even_w_in[j, :, EVEN_MAIN:], GATE_PAD) for j in range(n_even)]),
        "even_lam": jnp.stack([even_lam_q1, even_lam_k1, even_lam_q2, even_lam_k2], axis=1),
        "even_sub_g": even_attn_sub_g[:, :, None],
        "even_gk_w": gk_w,
        "even_gk_b": jnp.stack([even_gk_b_f, even_gk_b_b], axis=1)[:, :, None, :],
        "even_gla_norm_g": even_gla_norm_g[:, None, :],
        "even_w_out": even_w_out.astype(BF16),
        "odd_w_main": odd_w_in[:, :, :ODD_MAIN].astype(BF16),
        "odd_w_gate": jnp.stack([_pad_cols(odd_w_in[j, :, ODD_MAIN:], GATE_PAD) for j in range(n_odd)]),
        "odd_w_gate_t": jnp.swapaxes(odd_w_in[:, :, ODD_MAIN:], 1, 2),
        "odd_gate_b": odd_gate_b,
        "odd_norm_g": odd_norm_g[:, None, :],
        "odd_w_out": odd_w_out.astype(BF16),
        "ffn_w_gu": ffn_w_gu.astype(BF16),
        "ffn_w_down": ffn_w_down.astype(BF16),
    }
    nbp = c_prompt.shape[0]
    mod = _modulation(jnp.concatenate([c_prompt, c_sample], axis=0), w_mod, b_mod)
    y_prompt = _trunk(x_prompt, mod[:, :nbp], p)
    y_sample = _trunk(x_sample, mod[:, nbp:], p)
    return (y_prompt, y_sample)
```

```python
import functools
import math

import jax
import jax.numpy as jnp
from jax import lax
from jax.experimental import pallas as pl
from jax.experimental.pallas import tpu as pltpu

F32 = jnp.float32
BF16 = jnp.bfloat16
HIGHEST = lax.Precision.HIGHEST

D_MODEL = 1024
DEPTH = 2
EPS = 1e-6
A_HEADS = 4
A_QK_DIM = 64
A_V_DIM = 128
B_HEADS = 4
B_K_DIM = 64
B_V_DIM = 128
B_GATE_RANK = 16
B_GATE_NORM = 16.0
B_CHUNK = 64
C_HEADS = 4
C_QK_DIM = 128
C_V_DIM = 256
C_CHUNK = 64
D_FF = 2816
EVEN_AV = 1024
EVEN_REST = 1536
EVEN_MAIN = 3072
EVEN_AK = 512
EVEN_BQ = 1536
EVEN_BV = 2048
EVEN_BG = 2560
ODD_MAIN = 3072
ODD_CO = 2048
GATE_PAD = 128
ONES_ROWS = 16
LOG2E = 1.4426950408889634
ATTN_SUB = 256
ATTN_AHEAD = 2
ATTN_STREAMS = 4

VMEM_LIMIT = 56 * 1024 * 1024

_NT = (((1,), (1,)), ((), ()))
_TN = (((0,), (0,)), ((), ()))


def _log_sigmoid(x):
    return jnp.minimum(x, 0.0) - jnp.log1p(jnp.exp(-jnp.abs(x)))


def _silu(x):
    return x * jax.nn.sigmoid(x)


def _split_bf16(x, terms):
    parts = []
    for _ in range(terms - 1):
        hi = x.astype(BF16)
        parts.append(hi)
        x = x - hi.astype(F32)
    parts.append(x.astype(BF16))
    return parts


SPLIT_TERMS = 2


def _dot_exact_lhs(a_bf16, x):
    return sum(jnp.dot(a_bf16, part, preferred_element_type=F32) for part in _split_bf16(x, SPLIT_TERMS))


def _dot_exact_rhs(x, a_bf16):
    return sum(jnp.dot(part, a_bf16, preferred_element_type=F32) for part in _split_bf16(x, SPLIT_TERMS))


def _params(sem):
    return pltpu.CompilerParams(dimension_semantics=sem, vmem_limit_bytes=VMEM_LIMIT)


def _mod_kernel(c_ref, w_ref, b_ref, o_ref):
    s = _silu(c_ref[...])
    o_ref[0] = jnp.dot(s, w_ref[0], precision=HIGHEST, preferred_element_type=F32) + b_ref[0]


def _modulation(c_all, w_mod, b_mod):
    nb = c_all.shape[0]
    tn = 1536
    return pl.pallas_call(
        _mod_kernel,
        grid=(DEPTH, 6 * D_MODEL // tn),
        in_specs=[
            pl.BlockSpec((nb, D_MODEL), lambda i, j: (0, 0)),
            pl.BlockSpec((1, D_MODEL, tn), lambda i, j: (i, 0, j)),
            pl.BlockSpec((1, 1, tn), lambda i, j: (i, 0, j)),
        ],
        out_specs=pl.BlockSpec((1, nb, tn), lambda i, j: (i, 0, j)),
        out_shape=jax.ShapeDtypeStruct((DEPTH, nb, 6 * D_MODEL), F32),
        compiler_params=_params(("parallel", "parallel")),
        name="modulation",
    )(c_all, w_mod, b_mod.reshape(DEPTH, 1, 6 * D_MODEL))


def _norm_mod(x, g, sh, sc):
    ms = jnp.mean(x * x, axis=-1, keepdims=True)
    y = x * lax.rsqrt(ms + EPS) * g
    return y * (1.0 + sc) + sh


def _inproj_kernel(x_ref, sh_ref, sc_ref, g_ref, w_ref, wg_ref, wt_ref, *rest, n_main, n_col, chunk, with_rows):
    if with_rows:
        z_ref, zg_ref, zt_ref = rest
    else:
        kf_ref, z_ref, zg_ref, zt_ref = rest
    h = _norm_mod(x_ref[0], g_ref[...], sh_ref[0], sc_ref[0])
    hb = h.astype(BF16)
    tm = h.shape[0]
    for j in range(n_main // n_col):
        zc = jnp.dot(hb, w_ref[:, j * n_col:(j + 1) * n_col], preferred_element_type=F32)
        if with_rows or j == 0:
            z_ref[0, :, j * n_col:(j + 1) * n_col] = zc.astype(BF16)
        elif j == 1:
            hw = 2 * A_QK_DIM
            dup = jnp.concatenate([zc[:, (i // 2) * hw:(i // 2 + 1) * hw] for i in range(2 * A_HEADS)], axis=1)
            pos = (lax.broadcasted_iota(jnp.int32, (tm, 1), 0) & (ATTN_SUB - 1)).astype(F32)
            feat = kf_ref[1:2, :] + kf_ref[2:3, :] * pos
            z_ref[0, :, n_col:3 * n_col] = jnp.where(kf_ref[0:1, :] > 0.0, dup, feat).astype(BF16)
        else:
            z_ref[0, :, (j + 1) * n_col:(j + 2) * n_col] = zc.astype(BF16)
    zg_ref[0] = jnp.dot(hb, wg_ref[...].astype(BF16), preferred_element_type=F32)
    if with_rows:
        for c in range(tm // chunk):
            zt_ref[0, c] = lax.dot_general(wt_ref[...].astype(BF16), hb[c * chunk:(c + 1) * chunk], _NT,
                                           preferred_element_type=F32)
    else:
        for c in range(h.shape[0] // ATTN_SUB):
            zt_ref[0, c] = lax.dot_general(wt_ref[...], hb[c * ATTN_SUB:(c + 1) * ATTN_SUB], _NT,
                                           preferred_element_type=F32).astype(BF16)


def _inproj(x, sh, sc, g, w_main, w_gate, w_t, k_feat=None, *, with_rows, tm=512, chunk=64):
    B, T, _ = x.shape
    n_main = w_main.shape[1]
    n_col = 512
    n_z = n_main if with_rows else n_main + n_col
    kern = functools.partial(_inproj_kernel, n_main=n_main, n_col=n_col, chunk=chunk, with_rows=with_rows)
    in_specs = [
        pl.BlockSpec((1, tm, D_MODEL), lambda b, i: (b, i, 0)),
        pl.BlockSpec((1, 1, D_MODEL), lambda b, i: (b, 0, 0)),
        pl.BlockSpec((1, 1, D_MODEL), lambda b, i: (b, 0, 0)),
        pl.BlockSpec((1, D_MODEL), lambda b, i: (0, 0)),
        pl.BlockSpec((D_MODEL, n_main), lambda b, i: (0, 0)),
        pl.BlockSpec((D_MODEL, GATE_PAD), lambda b, i: (0, 0)),
    ]
    out_specs = [
        pl.BlockSpec((1, tm, n_z), lambda b, i: (b, i, 0)),
        pl.BlockSpec((1, tm, GATE_PAD), lambda b, i: (b, i, 0)),
    ]
    out_shape = [
        jax.ShapeDtypeStruct((B, T, n_z), BF16),
        jax.ShapeDtypeStruct((B, T, GATE_PAD), F32),
    ]
    args = [x, sh, sc, g, w_main, w_gate, w_t]
    nt = w_t.shape[0]
    in_specs.append(pl.BlockSpec((nt, D_MODEL), lambda b, i: (0, 0)))
    if with_rows:
        out_specs.append(pl.BlockSpec((1, tm // chunk, nt, chunk), lambda b, i: (b, i, 0, 0)))
        out_shape.append(jax.ShapeDtypeStruct((B, T // chunk, nt, chunk), F32))
    else:
        args.append(k_feat)
        in_specs.append(pl.BlockSpec(k_feat.shape, lambda b, i: (0, 0)))
        out_specs.append(pl.BlockSpec((1, tm // ATTN_SUB, nt, ATTN_SUB), lambda b, i: (b, i, 0, 0)))
        out_shape.append(jax.ShapeDtypeStruct((B, T // ATTN_SUB, nt, ATTN_SUB), BF16))
    return pl.pallas_call(
        kern,
        grid=(B, T // tm),
        in_specs=in_specs,
        out_specs=out_specs,
        out_shape=out_shape,
        compiler_params=_params(("parallel", "parallel")),
        name="inproj_rows" if with_rows else "inproj",
    )(*args)


def _attn_kernel(slope_ref, q_ref, ka_ref, kb_ref, vt_ref, lamp_ref, subg_ref, o_ref, q_scr, *, n_sub, lam_init):
    h = pl.program_id(1)
    tq = sk = ATTN_SUB
    dv = A_V_DIM
    hw = 2 * A_QK_DIM
    slope2 = slope_ref[h, 0] + slope_ref[h, 1] + slope_ref[h, 2]
    streams = range(ATTN_STREAMS)
    qis = [pl.program_id(2) * ATTN_STREAMS + st for st in streams]

    lane = lax.broadcasted_iota(jnp.int32, (1, hw), 1)
    c_off = lax.broadcasted_iota(jnp.int32, (tq, hw), 0).astype(F32)
    for c in range(2):
        own = (lane < A_QK_DIM) if c == 0 else (lane >= A_QK_DIM)
        base = (1 - c) * A_QK_DIM
        q_feat = jnp.zeros((tq, hw), F32)
        for x in range(3):
            q_feat = jnp.where(lane == base + x, slope_ref[h, x], q_feat)
            q_feat = jnp.where(lane == base + 3 + x, -c_off, q_feat)
        for st in streams:
            qs = (q_ref[0, st * tq:(st + 1) * tq, :].astype(F32) * (A_QK_DIM ** -0.5 * LOG2E)).astype(BF16)
            for v, sign in enumerate((1.0, -1.0, 0.0)):
                q_scr[st, v, c] = jnp.where(own, qs, (sign * q_feat).astype(BF16))
    rel = (lax.broadcasted_iota(jnp.int32, (sk, tq), 0)
           - lax.broadcasted_iota(jnp.int32, (sk, tq), 1)).astype(F32)
    diag_bias = -slope2 * jnp.abs(rel)
    ones = jnp.ones((ONES_ROWS, sk), BF16)
    k_refs = (ka_ref, kb_ref)

    def scores(st, j):
        qi = qis[st]
        if j == 0:
            idx, const, v = qi, jnp.zeros((), F32), 2
        else:
            idx = qi + j
            idx = jnp.where(idx >= n_sub, idx - n_sub, idx)
            before = idx < qi
            const = jnp.where(before, 1.0, -1.0) * slope2 * ((idx - qi) * sk).astype(F32)
            v = jnp.where(before, 0, 1)
        rows = pl.ds(pl.multiple_of(idx * sk, sk), sk)
        s = [lax.dot_general(k_refs[c][0, rows, :], q_scr[st, v, c], _NT, preferred_element_type=F32)
             for c in range(2)]
        if j == 0:
            s = [s_c + diag_bias for s_c in s]
        s = jnp.concatenate(s, axis=1)
        return s, jnp.max(s, axis=0, keepdims=True) + const, const, idx

    def accumulate(acc, alpha, p, idx):
        vt1 = jnp.concatenate([vt_ref[0, idx], ones], axis=0)
        return alpha * acc + jnp.dot(vt1, p, preferred_element_type=F32)

    m = [jnp.full((1, 2 * tq), -jnp.inf, F32) for _ in streams]
    acc = [jnp.zeros((dv + ONES_ROWS, 2 * tq), F32) for _ in streams]
    ahead = [[scores(st, j) for j in range(min(ATTN_AHEAD, n_sub))] for st in streams]
    pending = [None for _ in streams]
    for j in range(n_sub):
        for st in streams:
            s, m_loc, const, idx = ahead[st].pop(0)
            if j + ATTN_AHEAD < n_sub:
                ahead[st].append(scores(st, j + ATTN_AHEAD))
            m_new = jnp.maximum(m[st], m_loc)
            alpha = jnp.exp2(m[st] - m_new)
            p = jnp.exp2(s - (m_new - const)).astype(BF16)
            m[st] = m_new
            if pending[st] is not None:
                acc[st] = accumulate(acc[st], *pending[st])
            pending[st] = (alpha, p, idx)
    lp = lamp_ref[...]
    lam = (jnp.exp(jnp.sum(lp[0:1] * lp[1:2], axis=-1, keepdims=True))
           - jnp.exp(jnp.sum(lp[2:3] * lp[3:4], axis=-1, keepdims=True)) + lam_init)
    for st in streams:
        a = accumulate(acc[st], *pending[st])
        on = a[:dv] / a[dv:dv + 1]
        o = on[:, :tq] - lam * on[:, tq:]
        ms = jnp.mean(o * o, axis=0, keepdims=True)
        y = o * lax.rsqrt(ms + EPS) * subg_ref[...] * (1.0 - lam_init)
        o_ref[0, st * tq:(st + 1) * tq, :] = y.T.astype(o_ref.dtype)


def _alibi_slope_terms():
    slope2 = jnp.exp2(-8.0 * jnp.arange(1, A_HEADS + 1, dtype=F32) / A_HEADS) * LOG2E
    s0 = slope2.astype(BF16).astype(F32)
    s1 = (slope2 - s0).astype(BF16).astype(F32)
    s2 = (slope2 - s0 - s1).astype(BF16).astype(F32)
    return jnp.stack([s0, s1, s2], axis=1)


def _alibi_key_features():
    hw = 2 * A_QK_DIM
    terms = _alibi_slope_terms()
    lane = jnp.arange(hw)
    rows = []
    for h in range(A_HEADS):
        for c in range(2):
            base = (1 - c) * A_QK_DIM
            keep = ((lane < A_QK_DIM) if c == 0 else (lane >= A_QK_DIM)).astype(F32)
            const = jnp.zeros((hw,), F32).at[base + 3:base + 6].set(terms[h])
            pos = jnp.zeros((hw,), F32).at[base:base + 3].set(1.0)
            rows.append(jnp.stack([keep, const, pos]))
    return jnp.concatenate(rows, axis=1)


def _diff_attention(z, vt, lam_params, sub_g_col, lam_init):
    B, T, _ = z.shape
    n_sub = T // ATTN_SUB
    slopes = _alibi_slope_terms()
    kern = functools.partial(_attn_kernel, n_sub=n_sub, lam_init=lam_init)
    hw = 2 * A_QK_DIM
    return pl.pallas_call(
        kern,
        grid=(B, A_HEADS, n_sub // ATTN_STREAMS),
        in_specs=[
            pl.BlockSpec(memory_space=pltpu.SMEM),
            pl.BlockSpec((1, ATTN_STREAMS * ATTN_SUB, hw), lambda b, h, i: (b, i, h)),
            pl.BlockSpec((1, T, hw), lambda b, h, i: (b, 0, EVEN_AK // hw + 2 * h)),
            pl.BlockSpec((1, T, hw), lambda b, h, i: (b, 0, EVEN_AK // hw + 2 * h + 1)),
            pl.BlockSpec((1, n_sub, A_V_DIM, ATTN_SUB), lambda b, h, i: (b, 0, h, 0)),
            pl.BlockSpec((4, A_QK_DIM), lambda b, h, i: (0, 0)),
            pl.BlockSpec((A_V_DIM, 1), lambda b, h, i: (0, 0)),
        ],
        out_specs=pl.BlockSpec((1, ATTN_STREAMS * ATTN_SUB, A_V_DIM), lambda b, h, i: (b, i, h)),
        out_shape=jax.ShapeDtypeStruct((B, T, A_HEADS * A_V_DIM), BF16),
        scratch_shapes=[pltpu.VMEM((ATTN_STREAMS, 3, 2, ATTN_SUB, hw), BF16)],
        compiler_params=_params(("parallel", "parallel", "parallel")),
        name="diff_attention",
    )(slopes, z, z, z, vt, lam_params, sub_g_col)


def _gla_kernel(qf_ref, kf_ref, vf_ref, lrf_ref, qb_ref, kb_ref, vb_ref, lrb_ref, gw_ref, gb_ref,
                of_ref, ob_ref, s_scr, *, tg):
    L = B_CHUNK
    nchunk = tg // L
    pw = 2 * B_K_DIM
    vw = 2 * B_V_DIM

    @pl.when(pl.program_id(1) == 0)
    def _():
        s_scr[...] = jnp.zeros(s_scr.shape, F32)

    tt = lax.broadcasted_iota(jnp.int32, (L, L), 0)
    ss = lax.broadcasted_iota(jnp.int32, (L, L), 1)
    lane = lax.broadcasted_iota(jnp.int32, (1, pw), 1)
    srow = lax.broadcasted_iota(jnp.int32, (vw, pw), 0)
    slane = lax.broadcasted_iota(jnp.int32, (vw, pw), 1)
    same_head = (srow >= B_V_DIM) == (slane >= B_K_DIM)
    zero_b = jnp.zeros((), BF16)

    dirs = ((qf_ref, kf_ref, vf_ref, lrf_ref, of_ref), (qb_ref, kb_ref, vb_ref, lrb_ref, ob_ref))
    keeps = (ss <= tt, ss >= tt)
    tris = [kp.astype(BF16) for kp in keeps]
    lgs = []
    for d in range(2):
        pre = jnp.dot(dirs[d][3][0].astype(BF16), gw_ref[d].astype(BF16), preferred_element_type=F32) + gb_ref[d]
        lgs.append(_log_sigmoid(pre) * (1.0 / B_GATE_NORM))
    np_ = B_HEADS // 2
    states = [[s_scr[d, p] for p in range(np_)] for d in range(2)]
    sels = (lane < B_K_DIM, lane >= B_K_DIM)
    csl = [slice(c * L, (c + 1) * L) for c in range(nchunk)]
    ksl = [slice(p * pw, (p + 1) * pw) for p in range(np_)]
    items = [(c, d) for c in range(nchunk) for d in range(2)]
    chains = [(c, d, p) for c, d in items for p in range(np_)]
    bs = {(c, d): _dot_exact_lhs(tris[d], lgs[d][csl[c]]) for c, d in items}
    b_ends = {(c, d): bs[c, d][L - 1:L] if d == 0 else bs[c, d][0:1] for c, d in items}
    qcs = {(c, d): dirs[d][0][0, csl[c], :].astype(F32) for c, d in items}
    kcs = {(c, d): dirs[d][1][0, csl[c], :].astype(F32) for c, d in items}
    qes = {i: (qcs[i] * (B_K_DIM ** -0.5) * jnp.exp(bs[i])).astype(BF16) for i in items}
    kes = {i: (kcs[i] * jnp.exp(-bs[i])).astype(BF16) for i in items}
    kds = {i: (kcs[i] * jnp.exp(b_ends[i] - bs[i])).astype(BF16) for i in items}
    decs = {i: jnp.exp(b_ends[i]) for i in items}
    v_ps = {(c, d, p): dirs[d][2][0, csl[c], p * vw:(p + 1) * vw] for c, d, p in chains}
    atts = {(c, d, p, hh): lax.dot_general(jnp.where(sels[hh], qes[c, d][:, ksl[p]], zero_b), kes[c, d][:, ksl[p]],
                                           _NT, preferred_element_type=F32)
            for c, d, p in chains for hh in range(2)}
    atts = {key: jnp.where(keeps[key[1]], a, 0.0).astype(BF16) for key, a in atts.items()}
    intras = {(c, d, p): jnp.concatenate(
        [jnp.dot(atts[c, d, p, hh], v_ps[c, d, p][:, hh * B_V_DIM:(hh + 1) * B_V_DIM],
                 preferred_element_type=F32) for hh in range(2)], axis=-1) for c, d, p in chains}
    uts = {(c, d, p): jnp.where(same_head, lax.dot_general(v_ps[c, d, p], kds[c, d][:, ksl[p]], _TN,
                                                           preferred_element_type=F32), 0.0)
           for c, d, p in chains}
    for ci in range(nchunk):
        for d in range(2):
            c = ci if d == 0 else nchunk - 1 - ci
            for p in range(np_):
                st = states[d][p]
                inter = lax.dot_general(qes[c, d][:, ksl[p]], st.astype(BF16), _NT, preferred_element_type=F32)
                dirs[d][4][0, csl[c], p * vw:(p + 1) * vw] = inter + intras[c, d, p]
                states[d][p] = decs[c, d][:, ksl[p]] * st + uts[c, d, p]
    for d in range(2):
        for p in range(np_):
            s_scr[d, p] = states[d][p]


def _gla(z, zlr, gw, gb, *, tg=1024):
    B, T, _ = z.shape
    nb = T // tg
    kw = B_HEADS * B_K_DIM
    vw = B_HEADS * B_V_DIM
    q_blk = EVEN_BQ // kw
    v_blk = EVEN_BV // vw
    fwd = lambda b, i: (b, i)
    bwd = lambda b, i: (b, nb - 1 - i)

    def specs(order):
        return [
            pl.BlockSpec((1, tg, kw), lambda b, i: (*order(b, i), q_blk)),
            pl.BlockSpec((1, tg, kw), lambda b, i: (*order(b, i), q_blk + 1)),
            pl.BlockSpec((1, tg, vw), lambda b, i: (*order(b, i), v_blk)),
            pl.BlockSpec((1, tg, GATE_PAD), lambda b, i: (*order(b, i), 0)),
        ]

    kern = functools.partial(_gla_kernel, tg=tg)
    return pl.pallas_call(
        kern,
        grid=(B, nb),
        in_specs=specs(fwd) + specs(bwd) + [
            pl.BlockSpec((2, GATE_PAD, kw), lambda b, i: (0, 0, 0)),
            pl.BlockSpec((2, 1, kw), lambda b, i: (0, 0, 0)),
        ],
        out_specs=[
            pl.BlockSpec((1, tg, vw), lambda b, i: (b, i, 0)),
            pl.BlockSpec((1, tg, vw), lambda b, i: (b, nb - 1 - i, 0)),
        ],
        out_shape=[jax.ShapeDtypeStruct((B, T, vw), F32)] * 2,
        scratch_shapes=[pltpu.VMEM((2, B_HEADS // 2, 2 * B_V_DIM, 2 * B_K_DIM), F32)],
        compiler_params=_params(("parallel", "arbitrary")),
        name="gla",
    )(z, z, z, zlr, z, z, z, zlr, gw, gb)


def _head_rmsnorm(o, g, width):
    parts = []
    for h in range(o.shape[-1] // width):
        oh = o[:, h * width:(h + 1) * width]
        ms = jnp.mean(oh * oh, axis=-1, keepdims=True)
        parts.append(oh * lax.rsqrt(ms + EPS) * g[:, h * width:(h + 1) * width])
    return jnp.concatenate(parts, axis=-1)


def _mix_ffn_kernel(*refs, even, final_norm, ff_chunk):
    if even:
        a_ref, of_ref, ob_ref, gate_ref = refs[:4]
        refs = refs[4:]
    else:
        hf_ref, hb_ref, gate_ref = refs[:3]
        refs = refs[3:]
    gn_ref, x_ref, g1_ref, wo_ref, sh_ref, sc_ref, g_ref, g2_ref, wgu_ref, wd_ref, fg_ref, o_ref = refs
    if even:
        o = _head_rmsnorm(of_ref[0] + ob_ref[0], gn_ref[...], B_V_DIM)
        b_out = (o * _silu(gate_ref[0].astype(F32))).astype(BF16)
        mix = jnp.concatenate([a_ref[0], b_out], axis=-1)
    else:
        ht = _head_rmsnorm(hf_ref[0] + hb_ref[0], gn_ref[...], C_V_DIM)
        mix = (jax.nn.sigmoid(gate_ref[0].astype(F32)) * ht).astype(BF16)
    x1 = x_ref[0] + g1_ref[0] * jnp.dot(mix, wo_ref[...], preferred_element_type=F32)
    hb = _norm_mod(x1, g_ref[...], sh_ref[0], sc_ref[0]).astype(BF16)
    acts = []
    for c in range(D_FF // ff_chunk):
        gate = jnp.dot(hb, wgu_ref[:, c * ff_chunk:(c + 1) * ff_chunk], preferred_element_type=F32)
        up = jnp.dot(hb, wgu_ref[:, D_FF + c * ff_chunk:D_FF + (c + 1) * ff_chunk], preferred_element_type=F32)
        acts.append((_silu(gate) * up).astype(BF16))
    act = jnp.concatenate(acts, axis=-1)
    y = x1 + g2_ref[0] * jnp.dot(act, wd_ref[...], preferred_element_type=F32)
    if final_norm:
        ms = jnp.mean(y * y, axis=-1, keepdims=True)
        y = y * lax.rsqrt(ms + EPS) * fg_ref[...]
    o_ref[0] = y


def _mix_ffn(mixer_outs, z, gate_blk, gn, x, g1, w_out, sh, sc, g, g2, w_gu, w_down, final_g,
             *, even, final_norm, tm=512, ff_chunk=256):
    B, T, _ = x.shape
    tok = lambda b, i: (b, i, 0)
    per_b = lambda b, i: (b, 0, 0)
    const = lambda b, i: (0, 0)
    resident = functools.partial(pl.BlockSpec, index_map=const, pipeline_mode=pl.Buffered(1))
    nm = mixer_outs[0].shape[-1]
    in_specs = [pl.BlockSpec((1, tm, nm), tok) for _ in mixer_outs]
    in_specs += [
        pl.BlockSpec((1, tm, nm), lambda b, i: (b, i, gate_blk)),
        pl.BlockSpec((1, nm), const),
        pl.BlockSpec((1, tm, D_MODEL), tok),
        pl.BlockSpec((1, 1, D_MODEL), per_b),
        resident(w_out.shape),
        pl.BlockSpec((1, 1, D_MODEL), per_b),
        pl.BlockSpec((1, 1, D_MODEL), per_b),
        pl.BlockSpec((1, D_MODEL), const),
        pl.BlockSpec((1, 1, D_MODEL), per_b),
        resident(w_gu.shape),
        resident(w_down.shape),
        pl.BlockSpec((1, D_MODEL), const),
    ]
    kern = functools.partial(_mix_ffn_kernel, even=even, final_norm=final_norm, ff_chunk=ff_chunk)
    return pl.pallas_call(
        kern,
        grid=(B, T // tm),
        in_specs=in_specs,
        out_specs=pl.BlockSpec((1, tm, D_MODEL), tok),
        out_shape=jax.ShapeDtypeStruct((B, T, D_MODEL), F32),
        compiler_params=_params(("parallel", "parallel")),
        name=("mix_ffn_even" if even else "mix_ffn_odd") + ("_final" if final_norm else ""),
    )(*mixer_outs, z, gn, x, g1, w_out, sh, sc, g, g2, w_gu, w_down, final_g)


def _mlstm_kernel(gbr_ref, gbc_ref, qf_ref, kf_ref, vf_ref, gcf_ref, grf_ref, qb_ref, kb_ref, vb_ref, gcb_ref,
                  grb_ref, of_ref, ob_ref, c_scr, m_scr, *, tg):
    L = C_CHUNK
    nchunk = tg // L
    dk, dv = C_QK_DIM, C_V_DIM

    @pl.when(pl.program_id(1) == 0)
    def _():
        c_scr[...] = jnp.zeros(c_scr.shape, F32)
        m_scr[...] = jnp.zeros(m_scr.shape, F32)

    tt = lax.broadcasted_iota(jnp.int32, (L, L), 0)
    ss = lax.broadcasted_iota(jnp.int32, (L, L), 1)

    dirs = ((qf_ref, kf_ref, vf_ref, gcf_ref, grf_ref, of_ref),
            (qb_ref, kb_ref, vb_ref, gcb_ref, grb_ref, ob_ref))
    keeps = (ss <= tt, ss >= tt)
    tri_col = [keeps[0].astype(BF16), keeps[1].astype(BF16)]
    tri_row = [keeps[1].astype(BF16), keeps[0].astype(BF16)]
    glane = lax.broadcasted_iota(jnp.int32, (1, GATE_PAD), 1)
    f_lane = jnp.logical_and((glane & C_HEADS) != 0, glane < 4 * C_HEADS)
    f_row = (lax.broadcasted_iota(jnp.int32, (4 * C_HEADS, 1), 0) & C_HEADS) != 0
    csl = [slice(c * L, (c + 1) * L) for c in range(nchunk)]
    items = [(c, d) for c in range(nchunk) for d in range(2)]
    chains = [(c, d, h) for c, d in items for h in range(C_HEADS)]
    fi = {(d, h): (2 * d + 1) * C_HEADS + h for d in range(2) for h in range(C_HEADS)}

    Gc = []
    for d in range(2):
        g = dirs[d][3][0] + gbr_ref[...]
        Gc.append(jnp.where(f_lane, _log_sigmoid(g), g))
    Gr = {}
    for c, d in items:
        g = dirs[d][4][0, c] + gbc_ref[...]
        Gr[c, d] = jnp.where(f_row, _log_sigmoid(g), g)
    Gi = [pltpu.roll(Gc[d], C_HEADS, axis=1) for d in range(2)]
    Gri = {i: jnp.roll(Gr[i], C_HEADS, axis=0) for i in items}
    Fc = {(c, d): _dot_exact_lhs(tri_col[d], Gc[d][csl[c]]) for c, d in items}
    Fr = {(c, d): _dot_exact_rhs(Gr[c, d], tri_row[d]) for c, d in items}
    FLl = {(c, d): Fc[c, d][L - 1:L] if d == 0 else Fc[c, d][0:1] for c, d in items}
    WL = {(c, d): FLl[c, d] - Fc[c, d] + Gi[d][csl[c]] for c, d in items}
    wl_max = {i: jnp.max(WL[i], axis=0, keepdims=True) for i in items}
    w_r = {i: Gri[i] - Fr[i] for i in items}
    neg = jnp.full((L, GATE_PAD), -jnp.inf, F32)
    P = {}
    for c, d in items:
        x = Gi[d][csl[c]] - Fc[c, d]
        sh = 1
        while sh < L:
            shifted = (jnp.concatenate([neg[:sh], x[:L - sh]], axis=0) if d == 0
                       else jnp.concatenate([x[sh:], neg[:sh]], axis=0))
            x = jnp.maximum(x, shifted)
            sh *= 2
        P[c, d] = x
    m_old_l, m_new_l = {}, {}
    ml = [m_scr[d] for d in range(2)]
    for ci in range(nchunk):
        for d in range(2):
            c = ci if d == 0 else nchunk - 1 - ci
            m_old_l[c, d] = ml[d]
            ml[d] = jnp.maximum(FLl[c, d] + ml[d], wl_max[c, d])
            m_new_l[c, d] = ml[d]
    MT = {i: Fc[i] + jnp.maximum(m_old_l[i], P[i]) for i in items}
    U = {i: Fc[i] - MT[i] for i in items}
    INTER = {i: jnp.exp(Fc[i] + m_old_l[i] - MT[i]) for i in items}
    EMT = {i: jnp.exp(-MT[i]) for i in items}
    WS = {i: jnp.exp(WL[i] - m_new_l[i]) for i in items}
    DEC = {i: jnp.exp(FLl[i] + m_old_l[i] - m_new_l[i]) for i in items}
    na = dv + dk
    col = lambda X, k_, n: jnp.broadcast_to(X[k_[0], k_[1]][:, fi[k_[1], k_[2]]:fi[k_[1], k_[2]] + 1], (L, n))
    qc = {(c, d, h): dirs[d][0][0, csl[c], h * dk:(h + 1) * dk] for c, d, h in chains}
    kf = {(c, d, h): dirs[d][1][0, csl[c], h * dk:(h + 1) * dk].astype(F32) * (dk ** -0.5) for c, d, h in chains}
    ones = jnp.ones((L, dk), BF16)
    va = {(c, d, h): jnp.concatenate([dirs[d][2][0, csl[c], h * dv:(h + 1) * dv], ones], axis=1)
          for c, d, h in chains}
    qk = {k_: lax.dot_general(qc[k_], kf[k_].astype(BF16), _NT, preferred_element_type=F32) for k_ in chains}
    W = {k_: (jnp.exp(jnp.where(keeps[k_[1]], col(U, k_, L) + w_r[k_[0], k_[1]][fi[k_[1], k_[2]]:fi[k_[1], k_[2]] + 1],
                                -jnp.inf)) * qk[k_]).astype(BF16) for k_ in chains}
    Wv = {k_: jnp.dot(W[k_], va[k_], preferred_element_type=F32) for k_ in chains}
    kw = {k_: (kf[k_] * col(WS, k_, dk)).astype(BF16) for k_ in chains}
    kv = {k_: lax.dot_general(kw[k_], va[k_], _TN, preferred_element_type=F32) for k_ in chains}
    inter = {k_: col(INTER, k_, dk) for k_ in chains}
    e_mt = {k_: col(EMT, k_, dk) for k_ in chains}
    dec = {(c, d, h): DEC[c, d][:, fi[d, h]:fi[d, h] + 1] for c, d, h in chains}
    Cs = {(d, h): c_scr[d, h] for d in range(2) for h in range(C_HEADS)}
    for ci in range(nchunk):
        for d in range(2):
            c = ci if d == 0 else nchunk - 1 - ci
            for h in range(C_HEADS):
                k_ = (c, d, h)
                C = Cs[d, h]
                g = inter[k_]
                num = (jnp.concatenate([g] * (na // dk), axis=1)
                       * jnp.dot(qc[k_], C.astype(BF16), preferred_element_type=F32) + Wv[k_])
                den = jnp.maximum(jnp.abs(num[:, dv:]), e_mt[k_])
                dirs[d][5][0, csl[c], h * dv:(h + 1) * dv] = num[:, :dv] / jnp.concatenate([den] * (dv // dk), axis=1)
                Cs[d, h] = dec[k_] * C + kv[k_]
    for d in range(2):
        for h in range(C_HEADS):
            c_scr[d, h] = Cs[d, h]
        m_scr[d] = ml[d]


def _mlstm(z, gcol, grow, gate_b, *, tg=512):
    B, T, _ = z.shape
    nb = T // tg
    qw = C_HEADS * C_QK_DIM
    vw = C_HEADS * C_V_DIM
    ng = grow.shape[2]
    fwd = lambda b, i: (b, i)
    bwd = lambda b, i: (b, nb - 1 - i)

    def specs(order):
        return [
            pl.BlockSpec((1, tg, qw), lambda b, i: (*order(b, i), 0)),
            pl.BlockSpec((1, tg, qw), lambda b, i: (*order(b, i), 1)),
            pl.BlockSpec((1, tg, vw), lambda b, i: (*order(b, i), 1)),
            pl.BlockSpec((1, tg, GATE_PAD), lambda b, i: (*order(b, i), 0)),
            pl.BlockSpec((1, tg // C_CHUNK, ng, C_CHUNK), lambda b, i: (*order(b, i), 0, 0)),
        ]

    kern = functools.partial(_mlstm_kernel, tg=tg)
    return pl.pallas_call(
        kern,
        grid=(B, nb),
        in_specs=[pl.BlockSpec((1, GATE_PAD), lambda b, i: (0, 0)),
                  pl.BlockSpec((ng, 1), lambda b, i: (0, 0))] + specs(fwd) + specs(bwd),
        out_specs=[
            pl.BlockSpec((1, tg, vw), lambda b, i: (b, i, 0)),
            pl.BlockSpec((1, tg, vw), lambda b, i: (b, nb - 1 - i, 0)),
        ],
        out_shape=[jax.ShapeDtypeStruct((B, T, vw), F32)] * 2,
        scratch_shapes=[
            pltpu.VMEM((2, C_HEADS, C_QK_DIM, C_V_DIM + C_QK_DIM), F32),
            pltpu.VMEM((2, 1, GATE_PAD), F32),
        ],
        compiler_params=_params(("parallel", "arbitrary")),
        name="mlstm",
    )(_pad_cols(gate_b[None, :], GATE_PAD), gate_b[:, None], z, z, z, gcol, grow, z, z, z, gcol, grow)


def _pad_cols(w, n):
    return jnp.pad(w, ((0, 0), (0, n - w.shape[1])))


def _trunk(x, mod, p):
    for i in range(DEPTH):
        sh1, sc1, g1, sh2, sc2, g2 = [m[:, None, :] for m in jnp.split(mod[i], 6, axis=-1)]
        if i % 2 == 0:
            j = i // 2
            lam_init = 0.8 - 0.6 * math.exp(-0.3 * i)
            z, zlr, vt = _inproj(x, sh1, sc1, p["norm1_g"][i], p["even_w_main"][j], p["even_w_lr"][j],
                                 p["even_w_vt"][j], _alibi_key_features(), with_rows=False)
            a = _diff_attention(z, vt, p["even_lam"][j], p["even_sub_g"][j], lam_init)
            of, ob = _gla(z, zlr, p["even_gk_w"][j], p["even_gk_b"][j])
            mixer, gate_blk, gn, w_out = (a, of, ob), EVEN_BG // a.shape[-1], p["even_gla_norm_g"][j], p["even_w_out"][j]
        else:
            j = i // 2
            z, gcol, grow = _inproj(x, sh1, sc1, p["norm1_g"][i], p["odd_w_main"][j], p["odd_w_gate"][j],
                                    p["odd_w_gate_t"][j], with_rows=True, chunk=C_CHUNK)
            hf, hb = _mlstm(z, gcol, grow, p["odd_gate_b"][j])
            mixer, gate_blk, gn, w_out = (hf, hb), ODD_CO // hf.shape[-1], p["odd_norm_g"][j], p["odd_w_out"][j]
        x = _mix_ffn(mixer, z, gate_blk, gn, x, g1, w_out, sh2, sc2, p["norm2_g"][i], g2, p["ffn_w_gu"][i],
                     p["ffn_w_down"][i], p["final_g"], even=(i % 2 == 0), final_norm=(i == DEPTH - 1))
    return x


def kernel(x_prompt, x_sample, c_prompt, c_sample, w_mod, b_mod, norm1_g, norm2_g, even_w_in, even_lam_q1, even_lam_k1, even_lam_q2, even_lam_k2, even_attn_sub_g, even_gk_w_f, even_gk_b_f, even_gk_w_b, even_gk_b_b, even_gla_norm_g, even_w_out, odd_w_in, odd_gate_b, odd_norm_g, odd_w_out, ffn_w_gu, ffn_w_down, final_g):
    n_even = even_w_in.shape[0]
    n_odd = odd_w_in.shape[0]
    r = B_GATE_RANK
    kw = B_HEADS * B_K_DIM
    gk_w = jnp.zeros((n_even, 2, GATE_PAD, kw), F32)
    gk_w = gk_w.at[:, 0, 0:r].set(even_gk_w_f).at[:, 1, r:2 * r].set(even_gk_w_b)
    p = {
        "norm1_g": norm1_g[:, None, :],
        "norm2_g": norm2_g[:, None, :],
        "final_g": final_g[None, :],
        "even_w_main": jnp.concatenate([even_w_in[:, :, :EVEN_AV], even_w_in[:, :, EVEN_REST:EVEN_MAIN]],
                                       axis=-1).astype(BF16),
        "even_w_vt": jnp.swapaxes(even_w_in[:, :, EVEN_AV:EVEN_REST], 1, 2).astype(BF16),
        "even_w_lr": jnp.stack([_pad_cols(even_w_in[j, :, EVEN_MAIN:], GATE_PAD) for j in range(n_even)]),
        "even_lam": jnp.stack([even_lam_q1, even_lam_k1, even_lam_q2, even_lam_k2], axis=1),
        "even_sub_g": even_attn_sub_g[:, :, None],
        "even_gk_w": gk_w,
        "even_gk_b": jnp.stack([even_gk_b_f, even_gk_b_b], axis=1)[:, :, None, :],
        "even_gla_norm_g": even_gla_norm_g[:, None, :],
        "even_w_out": even_w_out.astype(BF16),
        "odd_w_main": odd_w_in[:, :, :ODD_MAIN].astype(BF16),
        "odd_w_gate": jnp.stack([_pad_cols(odd_w_in[j, :, ODD_MAIN:], GATE_PAD) for j in range(n_odd)]),
        "odd_w_gate_t": jnp.swapaxes(odd_w_in[:, :, ODD_MAIN:], 1, 2),
        "odd_gate_b": odd_gate_b,
        "odd_norm_g": odd_norm_g[:, None, :],
        "odd_w_out": odd_w_out.astype(BF16),
        "ffn_w_gu": ffn_w_gu.astype(BF16),
        "ffn_w_down": ffn_w_down.astype(BF16),
    }
    nbp = c_prompt.shape[0]
    mod = _modulation(jnp.concatenate([c_prompt, c_sample], axis=0), w_mod, b_mod)
    y_prompt = _trunk(x_prompt, mod[:, :nbp], p)
    y_sample = _trunk(x_sample, mod[:, nbp:], p)
    return (y_prompt, y_sample)
```

```python
import functools
import math

import jax
import jax.numpy as jnp
from jax import lax
from jax.experimental import pallas as pl
from jax.experimental.pallas import tpu as pltpu

F32 = jnp.float32
BF16 = jnp.bfloat16
HIGHEST = lax.Precision.HIGHEST

D_MODEL = 1024
DEPTH = 2
EPS = 1e-6
A_HEADS = 4
A_QK_DIM = 64
A_V_DIM = 128
B_HEADS = 4
B_K_DIM = 64
B_V_DIM = 128
B_GATE_RANK = 16
B_GATE_NORM = 16.0
B_CHUNK = 64
C_HEADS = 4
C_QK_DIM = 128
C_V_DIM = 256
C_CHUNK = 64
D_FF = 2816
EVEN_AV = 1024
EVEN_REST = 1536
EVEN_MAIN = 3072
EVEN_AK = 512
EVEN_BQ = 1536
EVEN_BV = 2048
EVEN_BG = 2560
ODD_MAIN = 3072
ODD_CO = 2048
GATE_PAD = 128
ONES_ROWS = 16
LOG2E = 1.4426950408889634
ATTN_SUB = 256
ATTN_AHEAD = 1
ATTN_STREAMS = 4

VMEM_LIMIT = 56 * 1024 * 1024

_NT = (((1,), (1,)), ((), ()))
_TN = (((0,), (0,)), ((), ()))


def _log_sigmoid(x):
    return jnp.minimum(x, 0.0) - jnp.log1p(jnp.exp(-jnp.abs(x)))


def _silu(x):
    return x * jax.nn.sigmoid(x)


def _split_bf16(x, terms):
    parts = []
    for _ in range(terms - 1):
        hi = x.astype(BF16)
        parts.append(hi)
        x = x - hi.astype(F32)
    parts.append(x.astype(BF16))
    return parts


SPLIT_TERMS = 2


def _dot_exact_lhs(a_bf16, x):
    return sum(jnp.dot(a_bf16, part, preferred_element_type=F32) for part in _split_bf16(x, SPLIT_TERMS))


def _dot_exact_rhs(x, a_bf16):
    return sum(jnp.dot(part, a_bf16, preferred_element_type=F32) for part in _split_bf16(x, SPLIT_TERMS))


def _params(sem):
    return pltpu.CompilerParams(dimension_semantics=sem, vmem_limit_bytes=VMEM_LIMIT)


def _mod_kernel(c_ref, w_ref, b_ref, o_ref):
    s = _silu(c_ref[...])
    o_ref[0] = jnp.dot(s, w_ref[0], precision=HIGHEST, preferred_element_type=F32) + b_ref[0]


def _modulation(c_all, w_mod, b_mod):
    nb = c_all.shape[0]
    tn = 1536
    return pl.pallas_call(
        _mod_kernel,
        grid=(DEPTH, 6 * D_MODEL // tn),
        in_specs=[
            pl.BlockSpec((nb, D_MODEL), lambda i, j: (0, 0)),
            pl.BlockSpec((1, D_MODEL, tn), lambda i, j: (i, 0, j)),
            pl.BlockSpec((1, 1, tn), lambda i, j: (i, 0, j)),
        ],
        out_specs=pl.BlockSpec((1, nb, tn), lambda i, j: (i, 0, j)),
        out_shape=jax.ShapeDtypeStruct((DEPTH, nb, 6 * D_MODEL), F32),
        compiler_params=_params(("parallel", "parallel")),
        name="modulation",
    )(c_all, w_mod, b_mod.reshape(DEPTH, 1, 6 * D_MODEL))


def _norm_mod(x, g, sh, sc):
    ms = jnp.mean(x * x, axis=-1, keepdims=True)
    y = x * lax.rsqrt(ms + EPS) * g
    return y * (1.0 + sc) + sh


def _inproj_kernel(x_ref, sh_ref, sc_ref, g_ref, w_ref, wg_ref, wt_ref, *rest, n_main, n_col, chunk, with_rows):
    if with_rows:
        z_ref, zg_ref, zt_ref = rest
    else:
        kf_ref, z_ref, zg_ref, zt_ref = rest
    h = _norm_mod(x_ref[0], g_ref[...], sh_ref[0], sc_ref[0])
    hb = h.astype(BF16)
    tm = h.shape[0]
    for j in range(n_main // n_col):
        zc = jnp.dot(hb, w_ref[:, j * n_col:(j + 1) * n_col], preferred_element_type=F32)
        if with_rows or j == 0:
            z_ref[0, :, j * n_col:(j + 1) * n_col] = zc.astype(BF16)
        elif j == 1:
            hw = 2 * A_QK_DIM
            dup = jnp.concatenate([zc[:, (i // 2) * hw:(i // 2 + 1) * hw] for i in range(2 * A_HEADS)], axis=1)
            pos = (lax.broadcasted_iota(jnp.int32, (tm, 1), 0) & (ATTN_SUB - 1)).astype(F32)
            feat = kf_ref[1:2, :] + kf_ref[2:3, :] * pos
            z_ref[0, :, n_col:3 * n_col] = jnp.where(kf_ref[0:1, :] > 0.0, dup, feat).astype(BF16)
        else:
            z_ref[0, :, (j + 1) * n_col:(j + 2) * n_col] = zc.astype(BF16)
    zg_ref[0] = jnp.dot(hb, wg_ref[...].astype(BF16), preferred_element_type=F32)
    if with_rows:
        for c in range(tm // chunk):
            zt_ref[0, c] = lax.dot_general(wt_ref[...].astype(BF16), hb[c * chunk:(c + 1) * chunk], _NT,
                                           preferred_element_type=F32)
    else:
        for c in range(h.shape[0] // ATTN_SUB):
            zt_ref[0, c] = lax.dot_general(wt_ref[...], hb[c * ATTN_SUB:(c + 1) * ATTN_SUB], _NT,
                                           preferred_element_type=F32).astype(BF16)


def _inproj(x, sh, sc, g, w_main, w_gate, w_t, k_feat=None, *, with_rows, tm=512, chunk=64):
    B, T, _ = x.shape
    n_main = w_main.shape[1]
    n_col = 512
    n_z = n_main if with_rows else n_main + n_col
    kern = functools.partial(_inproj_kernel, n_main=n_main, n_col=n_col, chunk=chunk, with_rows=with_rows)
    in_specs = [
        pl.BlockSpec((1, tm, D_MODEL), lambda b, i: (b, i, 0)),
        pl.BlockSpec((1, 1, D_MODEL), lambda b, i: (b, 0, 0)),
        pl.BlockSpec((1, 1, D_MODEL), lambda b, i: (b, 0, 0)),
        pl.BlockSpec((1, D_MODEL), lambda b, i: (0, 0)),
        pl.BlockSpec((D_MODEL, n_main), lambda b, i: (0, 0)),
        pl.BlockSpec((D_MODEL, GATE_PAD), lambda b, i: (0, 0)),
    ]
    out_specs = [
        pl.BlockSpec((1, tm, n_z), lambda b, i: (b, i, 0)),
        pl.BlockSpec((1, tm, GATE_PAD), lambda b, i: (b, i, 0)),
    ]
    out_shape = [
        jax.ShapeDtypeStruct((B, T, n_z), BF16),
        jax.ShapeDtypeStruct((B, T, GATE_PAD), F32),
    ]
    args = [x, sh, sc, g, w_main, w_gate, w_t]
    nt = w_t.shape[0]
    in_specs.append(pl.BlockSpec((nt, D_MODEL), lambda b, i: (0, 0)))
    if with_rows:
        out_specs.append(pl.BlockSpec((1, tm // chunk, nt, chunk), lambda b, i: (b, i, 0, 0)))
        out_shape.append(jax.ShapeDtypeStruct((B, T // chunk, nt, chunk), F32))
    else:
        args.append(k_feat)
        in_specs.append(pl.BlockSpec(k_feat.shape, lambda b, i: (0, 0)))
        out_specs.append(pl.BlockSpec((1, tm // ATTN_SUB, nt, ATTN_SUB), lambda b, i: (b, i, 0, 0)))
        out_shape.append(jax.ShapeDtypeStruct((B, T // ATTN_SUB, nt, ATTN_SUB), BF16))
    return pl.pallas_call(
        kern,
        grid=(B, T // tm),
        in_specs=in_specs,
        out_specs=out_specs,
        out_shape=out_shape,
        compiler_params=_params(("parallel", "parallel")),
        name="inproj_rows" if with_rows else "inproj",
    )(*args)


def _attn_kernel(slope_ref, q_ref, ka_ref, kb_ref, vt_ref, lamp_ref, subg_ref, o_ref, q_scr, *, n_sub, lam_init):
    h = pl.program_id(1)
    tq = sk = ATTN_SUB
    dv = A_V_DIM
    hw = 2 * A_QK_DIM
    slope2 = slope_ref[h, 0] + slope_ref[h, 1] + slope_ref[h, 2]
    streams = range(ATTN_STREAMS)
    qis = [pl.program_id(2) * ATTN_STREAMS + st for st in streams]

    lane = lax.broadcasted_iota(jnp.int32, (1, hw), 1)
    c_off = lax.broadcasted_iota(jnp.int32, (tq, hw), 0).astype(F32)
    for c in range(2):
        own = (lane < A_QK_DIM) if c == 0 else (lane >= A_QK_DIM)
        base = (1 - c) * A_QK_DIM
        q_feat = jnp.zeros((tq, hw), F32)
        for x in range(3):
            q_feat = jnp.where(lane == base + x, slope_ref[h, x], q_feat)
            q_feat = jnp.where(lane == base + 3 + x, -c_off, q_feat)
        for st in streams:
            qs = (q_ref[0, st * tq:(st + 1) * tq, :].astype(F32) * (A_QK_DIM ** -0.5 * LOG2E)).astype(BF16)
            for v, sign in enumerate((1.0, -1.0, 0.0)):
                q_scr[st, v, c] = jnp.where(own, qs, (sign * q_feat).astype(BF16))
    rel = (lax.broadcasted_iota(jnp.int32, (sk, tq), 0)
           - lax.broadcasted_iota(jnp.int32, (sk, tq), 1)).astype(F32)
    diag_bias = -slope2 * jnp.abs(rel)
    ones = jnp.ones((ONES_ROWS, sk), BF16)
    k_refs = (ka_ref, kb_ref)

    def scores(st, j):
        qi = qis[st]
        if j == 0:
            idx, const, v = qi, jnp.zeros((), F32), 2
        else:
            idx = qi + j
            idx = jnp.where(idx >= n_sub, idx - n_sub, idx)
            before = idx < qi
            const = jnp.where(before, 1.0, -1.0) * slope2 * ((idx - qi) * sk).astype(F32)
            v = jnp.where(before, 0, 1)
        rows = pl.ds(pl.multiple_of(idx * sk, sk), sk)
        s = [lax.dot_general(k_refs[c][0, rows, :], q_scr[st, v, c], _NT, preferred_element_type=F32)
             for c in range(2)]
        if j == 0:
            s = [s_c + diag_bias for s_c in s]
        s = jnp.concatenate(s, axis=1)
        return s, jnp.max(s, axis=0, keepdims=True) + const, const, idx

    def accumulate(acc, alpha, p, idx):
        vt1 = jnp.concatenate([vt_ref[0, idx], ones], axis=0)
        return alpha * acc + jnp.dot(vt1, p, preferred_element_type=F32)

    m = [jnp.full((1, 2 * tq), -jnp.inf, F32) for _ in streams]
    acc = [jnp.zeros((dv + ONES_ROWS, 2 * tq), F32) for _ in streams]
    ahead = [[scores(st, j) for j in range(min(ATTN_AHEAD, n_sub))] for st in streams]
    pending = [None for _ in streams]
    for j in range(n_sub):
        cur = [ahead[st].pop(0) for st in streams]
        if j + ATTN_AHEAD < n_sub:
            for st in streams:
                ahead[st].append(scores(st, j + ATTN_AHEAD))
        for st in streams:
            s, m_loc, const, idx = cur[st]
            m_new = jnp.maximum(m[st], m_loc)
            alpha = jnp.exp2(m[st] - m_new)
            p = jnp.exp2(s - (m_new - const)).astype(BF16)
            m[st] = m_new
            cur[st] = (alpha, p, idx)
        for st in streams:
            if pending[st] is not None:
                acc[st] = accumulate(acc[st], *pending[st])
            pending[st] = cur[st]
    lp = lamp_ref[...]
    lam = (jnp.exp(jnp.sum(lp[0:1] * lp[1:2], axis=-1, keepdims=True))
           - jnp.exp(jnp.sum(lp[2:3] * lp[3:4], axis=-1, keepdims=True)) + lam_init)
    for st in streams:
        a = accumulate(acc[st], *pending[st])
        on = a[:dv] / a[dv:dv + 1]
        o = on[:, :tq] - lam * on[:, tq:]
        ms = jnp.mean(o * o, axis=0, keepdims=True)
        y = o * lax.rsqrt(ms + EPS) * subg_ref[...] * (1.0 - lam_init)
        o_ref[0, st * tq:(st + 1) * tq, :] = y.T.astype(o_ref.dtype)


def _alibi_slope_terms():
    slope2 = jnp.exp2(-8.0 * jnp.arange(1, A_HEADS + 1, dtype=F32) / A_HEADS) * LOG2E
    s0 = slope2.astype(BF16).astype(F32)
    s1 = (slope2 - s0).astype(BF16).astype(F32)
    s2 = (slope2 - s0 - s1).astype(BF16).astype(F32)
    return jnp.stack([s0, s1, s2], axis=1)


def _alibi_key_features():
    hw = 2 * A_QK_DIM
    terms = _alibi_slope_terms()
    lane = jnp.arange(hw)
    rows = []
    for h in range(A_HEADS):
        for c in range(2):
            base = (1 - c) * A_QK_DIM
            keep = ((lane < A_QK_DIM) if c == 0 else (lane >= A_QK_DIM)).astype(F32)
            const = jnp.zeros((hw,), F32).at[base + 3:base + 6].set(terms[h])
            pos = jnp.zeros((hw,), F32).at[base:base + 3].set(1.0)
            rows.append(jnp.stack([keep, const, pos]))
    return jnp.concatenate(rows, axis=1)


def _diff_attention(z, vt, lam_params, sub_g_col, lam_init):
    B, T, _ = z.shape
    n_sub = T // ATTN_SUB
    slopes = _alibi_slope_terms()
    kern = functools.partial(_attn_kernel, n_sub=n_sub, lam_init=lam_init)
    hw = 2 * A_QK_DIM
    return pl.pallas_call(
        kern,
        grid=(B, A_HEADS, n_sub // ATTN_STREAMS),
        in_specs=[
            pl.BlockSpec(memory_space=pltpu.SMEM),
            pl.BlockSpec((1, ATTN_STREAMS * ATTN_SUB, hw), lambda b, h, i: (b, i, h)),
            pl.BlockSpec((1, T, hw), lambda b, h, i: (b, 0, EVEN_AK // hw + 2 * h)),
            pl.BlockSpec((1, T, hw), lambda b, h, i: (b, 0, EVEN_AK // hw + 2 * h + 1)),
            pl.BlockSpec((1, n_sub, A_V_DIM, ATTN_SUB), lambda b, h, i: (b, 0, h, 0)),
            pl.BlockSpec((4, A_QK_DIM), lambda b, h, i: (0, 0)),
            pl.BlockSpec((A_V_DIM, 1), lambda b, h, i: (0, 0)),
        ],
        out_specs=pl.BlockSpec((1, ATTN_STREAMS * ATTN_SUB, A_V_DIM), lambda b, h, i: (b, i, h)),
        out_shape=jax.ShapeDtypeStruct((B, T, A_HEADS * A_V_DIM), BF16),
        scratch_shapes=[pltpu.VMEM((ATTN_STREAMS, 3, 2, ATTN_SUB, hw), BF16)],
        compiler_params=_params(("parallel", "parallel", "parallel")),
        name="diff_attention",
    )(slopes, z, z, z, vt, lam_params, sub_g_col)


def _gla_kernel(qf_ref, kf_ref, vf_ref, lrf_ref, qb_ref, kb_ref, vb_ref, lrb_ref, gw_ref, gb_ref,
                of_ref, ob_ref, s_scr, *, tg):
    L = B_CHUNK
    nchunk = tg // L
    pw = 2 * B_K_DIM
    vw = 2 * B_V_DIM

    @pl.when(pl.program_id(1) == 0)
    def _():
        s_scr[...] = jnp.zeros(s_scr.shape, F32)

    tt = lax.broadcasted_iota(jnp.int32, (L, L), 0)
    ss = lax.broadcasted_iota(jnp.int32, (L, L), 1)
    lane = lax.broadcasted_iota(jnp.int32, (1, pw), 1)
    srow = lax.broadcasted_iota(jnp.int32, (vw, pw), 0)
    slane = lax.broadcasted_iota(jnp.int32, (vw, pw), 1)
    same_head = (srow >= B_V_DIM) == (slane >= B_K_DIM)
    zero_b = jnp.zeros((), BF16)

    dirs = ((qf_ref, kf_ref, vf_ref, lrf_ref, of_ref), (qb_ref, kb_ref, vb_ref, lrb_ref, ob_ref))
    keeps = (ss <= tt, ss >= tt)
    tris = [kp.astype(BF16) for kp in keeps]
    lgs = []
    for d in range(2):
        pre = jnp.dot(dirs[d][3][0].astype(BF16), gw_ref[d].astype(BF16), preferred_element_type=F32) + gb_ref[d]
        lgs.append(_log_sigmoid(pre) * (1.0 / B_GATE_NORM))
    np_ = B_HEADS // 2
    states = [[s_scr[d, p] for p in range(np_)] for d in range(2)]
    sels = (lane < B_K_DIM, lane >= B_K_DIM)
    csl = [slice(c * L, (c + 1) * L) for c in range(nchunk)]
    ksl = [slice(p * pw, (p + 1) * pw) for p in range(np_)]
    items = [(c, d) for c in range(nchunk) for d in range(2)]
    chains = [(c, d, p) for c, d in items for p in range(np_)]
    bs = {(c, d): _dot_exact_lhs(tris[d], lgs[d][csl[c]]) for c, d in items}
    b_ends = {(c, d): bs[c, d][L - 1:L] if d == 0 else bs[c, d][0:1] for c, d in items}
    qcs = {(c, d): dirs[d][0][0, csl[c], :].astype(F32) for c, d in items}
    kcs = {(c, d): dirs[d][1][0, csl[c], :].astype(F32) for c, d in items}
    qes = {i: (qcs[i] * (B_K_DIM ** -0.5) * jnp.exp(bs[i])).astype(BF16) for i in items}
    kes = {i: (kcs[i] * jnp.exp(-bs[i])).astype(BF16) for i in items}
    kds = {i: (kcs[i] * jnp.exp(b_ends[i] - bs[i])).astype(BF16) for i in items}
    decs = {i: jnp.exp(b_ends[i]) for i in items}
    v_ps = {(c, d, p): dirs[d][2][0, csl[c], p * vw:(p + 1) * vw] for c, d, p in chains}
    atts = {(c, d, p, hh): lax.dot_general(jnp.where(sels[hh], qes[c, d][:, ksl[p]], zero_b), kes[c, d][:, ksl[p]],
                                           _NT, preferred_element_type=F32)
            for c, d, p in chains for hh in range(2)}
    atts = {key: jnp.where(keeps[key[1]], a, 0.0).astype(BF16) for key, a in atts.items()}
    intras = {(c, d, p): jnp.concatenate(
        [jnp.dot(atts[c, d, p, hh], v_ps[c, d, p][:, hh * B_V_DIM:(hh + 1) * B_V_DIM],
                 preferred_element_type=F32) for hh in range(2)], axis=-1) for c, d, p in chains}
    uts = {(c, d, p): jnp.where(same_head, lax.dot_general(v_ps[c, d, p], kds[c, d][:, ksl[p]], _TN,
                                                           preferred_element_type=F32), 0.0)
           for c, d, p in chains}
    for ci in range(nchunk):
        for d in range(2):
            c = ci if d == 0 else nchunk - 1 - ci
            for p in range(np_):
                st = states[d][p]
                inter = lax.dot_general(qes[c, d][:, ksl[p]], st.astype(BF16), _NT, preferred_element_type=F32)
                dirs[d][4][0, csl[c], p * vw:(p + 1) * vw] = inter + intras[c, d, p]
                states[d][p] = decs[c, d][:, ksl[p]] * st + uts[c, d, p]
    for d in range(2):
        for p in range(np_):
            s_scr[d, p] = states[d][p]


def _gla(z, zlr, gw, gb, *, tg=1024):
    B, T, _ = z.shape
    nb = T // tg
    kw = B_HEADS * B_K_DIM
    vw = B_HEADS * B_V_DIM
    q_blk = EVEN_BQ // kw
    v_blk = EVEN_BV // vw
    fwd = lambda b, i: (b, i)
    bwd = lambda b, i: (b, nb - 1 - i)

    def specs(order):
        return [
            pl.BlockSpec((1, tg, kw), lambda b, i: (*order(b, i), q_blk)),
            pl.BlockSpec((1, tg, kw), lambda b, i: (*order(b, i), q_blk + 1)),
            pl.BlockSpec((1, tg, vw), lambda b, i: (*order(b, i), v_blk)),
            pl.BlockSpec((1, tg, GATE_PAD), lambda b, i: (*order(b, i), 0)),
        ]

    kern = functools.partial(_gla_kernel, tg=tg)
    return pl.pallas_call(
        kern,
        grid=(B, nb),
        in_specs=specs(fwd) + specs(bwd) + [
            pl.BlockSpec((2, GATE_PAD, kw), lambda b, i: (0, 0, 0)),
            pl.BlockSpec((2, 1, kw), lambda b, i: (0, 0, 0)),
        ],
        out_specs=[
            pl.BlockSpec((1, tg, vw), lambda b, i: (b, i, 0)),
            pl.BlockSpec((1, tg, vw), lambda b, i: (b, nb - 1 - i, 0)),
        ],
        out_shape=[jax.ShapeDtypeStruct((B, T, vw), F32)] * 2,
        scratch_shapes=[pltpu.VMEM((2, B_HEADS // 2, 2 * B_V_DIM, 2 * B_K_DIM), F32)],
        compiler_params=_params(("parallel", "arbitrary")),
        name="gla",
    )(z, z, z, zlr, z, z, z, zlr, gw, gb)


def _head_rmsnorm(o, g, width):
    parts = []
    for h in range(o.shape[-1] // width):
        oh = o[:, h * width:(h + 1) * width]
        ms = jnp.mean(oh * oh, axis=-1, keepdims=True)
        parts.append(oh * lax.rsqrt(ms + EPS) * g[:, h * width:(h + 1) * width])
    return jnp.concatenate(parts, axis=-1)


def _mix_ffn_kernel(*refs, even, final_norm, ff_chunk, row_tile):
    if even:
        a_ref, of_ref, ob_ref, gate_ref = refs[:4]
        refs = refs[4:]
    else:
        hf_ref, hb_ref, gate_ref = refs[:3]
        refs = refs[3:]
    gn_ref, x_ref, g1_ref, wo_ref, sh_ref, sc_ref, g_ref, g2_ref, wgu_ref, wd_ref, fg_ref, o_ref = refs
    tm = x_ref.shape[1]
    for r in range(tm // row_tile):
        rs = slice(r * row_tile, (r + 1) * row_tile)
        if even:
            o = _head_rmsnorm(of_ref[0, rs, :] + ob_ref[0, rs, :], gn_ref[...], B_V_DIM)
            b_out = (o * _silu(gate_ref[0, rs, :].astype(F32))).astype(BF16)
            mix = jnp.concatenate([a_ref[0, rs, :], b_out], axis=-1)
        else:
            ht = _head_rmsnorm(hf_ref[0, rs, :] + hb_ref[0, rs, :], gn_ref[...], C_V_DIM)
            mix = (jax.nn.sigmoid(gate_ref[0, rs, :].astype(F32)) * ht).astype(BF16)
        x1 = x_ref[0, rs, :] + g1_ref[0] * jnp.dot(mix, wo_ref[...], preferred_element_type=F32)
        hb = _norm_mod(x1, g_ref[...], sh_ref[0], sc_ref[0]).astype(BF16)
        acts = []
        for c in range(D_FF // ff_chunk):
            gate = jnp.dot(hb, wgu_ref[:, c * ff_chunk:(c + 1) * ff_chunk], preferred_element_type=F32)
            up = jnp.dot(hb, wgu_ref[:, D_FF + c * ff_chunk:D_FF + (c + 1) * ff_chunk],
                         preferred_element_type=F32)
            acts.append((_silu(gate) * up).astype(BF16))
        act = jnp.concatenate(acts, axis=-1)
        y = x1 + g2_ref[0] * jnp.dot(act, wd_ref[...], preferred_element_type=F32)
        if final_norm:
            ms = jnp.mean(y * y, axis=-1, keepdims=True)
            y = y * lax.rsqrt(ms + EPS) * fg_ref[...]
        o_ref[0, rs, :] = y


def _mix_ffn(mixer_outs, z, gate_blk, gn, x, g1, w_out, sh, sc, g, g2, w_gu, w_down, final_g,
             *, even, final_norm, tm=512, ff_chunk=256, row_tile=512):
    B, T, _ = x.shape
    tok = lambda b, i: (b, i, 0)
    per_b = lambda b, i: (b, 0, 0)
    const = lambda b, i: (0, 0)
    resident = functools.partial(pl.BlockSpec, index_map=const, pipeline_mode=pl.Buffered(1))
    nm = mixer_outs[0].shape[-1]
    in_specs = [pl.BlockSpec((1, tm, nm), tok) for _ in mixer_outs]
    in_specs += [
        pl.BlockSpec((1, tm, nm), lambda b, i: (b, i, gate_blk)),
        pl.BlockSpec((1, nm), const),
        pl.BlockSpec((1, tm, D_MODEL), tok),
        pl.BlockSpec((1, 1, D_MODEL), per_b),
        resident(w_out.shape),
        pl.BlockSpec((1, 1, D_MODEL), per_b),
        pl.BlockSpec((1, 1, D_MODEL), per_b),
        pl.BlockSpec((1, D_MODEL), const),
        pl.BlockSpec((1, 1, D_MODEL), per_b),
        resident(w_gu.shape),
        resident(w_down.shape),
        pl.BlockSpec((1, D_MODEL), const),
    ]
    kern = functools.partial(_mix_ffn_kernel, even=even, final_norm=final_norm, ff_chunk=ff_chunk,
                             row_tile=row_tile)
    return pl.pallas_call(
        kern,
        grid=(B, T // tm),
        in_specs=in_specs,
        out_specs=pl.BlockSpec((1, tm, D_MODEL), tok),
        out_shape=jax.ShapeDtypeStruct((B, T, D_MODEL), F32),
        compiler_params=_params(("parallel", "parallel")),
        name=("mix_ffn_even" if even else "mix_ffn_odd") + ("_final" if final_norm else ""),
    )(*mixer_outs, z, gn, x, g1, w_out, sh, sc, g, g2, w_gu, w_down, final_g)


def _mlstm_kernel(gbr_ref, gbc_ref, qf_ref, kf_ref, vf_ref, gcf_ref, grf_ref, qb_ref, kb_ref, vb_ref, gcb_ref,
                  grb_ref, of_ref, ob_ref, c_scr, m_scr, *, tg):
    L = C_CHUNK
    nchunk = tg // L
    dk, dv = C_QK_DIM, C_V_DIM

    @pl.when(pl.program_id(1) == 0)
    def _():
        c_scr[...] = jnp.zeros(c_scr.shape, F32)
        m_scr[...] = jnp.zeros(m_scr.shape, F32)

    tt = lax.broadcasted_iota(jnp.int32, (L, L), 0)
    ss = lax.broadcasted_iota(jnp.int32, (L, L), 1)

    dirs = ((qf_ref, kf_ref, vf_ref, gcf_ref, grf_ref, of_ref),
            (qb_ref, kb_ref, vb_ref, gcb_ref, grb_ref, ob_ref))
    keeps = (ss <= tt, ss >= tt)
    tri_col = [keeps[0].astype(BF16), keeps[1].astype(BF16)]
    tri_row = [keeps[1].astype(BF16), keeps[0].astype(BF16)]
    glane = lax.broadcasted_iota(jnp.int32, (1, GATE_PAD), 1)
    f_lane = jnp.logical_and((glane & C_HEADS) != 0, glane < 4 * C_HEADS)
    f_row = (lax.broadcasted_iota(jnp.int32, (4 * C_HEADS, 1), 0) & C_HEADS) != 0
    csl = [slice(c * L, (c + 1) * L) for c in range(nchunk)]
    items = [(c, d) for c in range(nchunk) for d in range(2)]
    chains = [(c, d, h) for c, d in items for h in range(C_HEADS)]
    fi = {(d, h): (2 * d + 1) * C_HEADS + h for d in range(2) for h in range(C_HEADS)}

    Gc = []
    for d in range(2):
        g = dirs[d][3][0] + gbr_ref[...]
        Gc.append(jnp.where(f_lane, _log_sigmoid(g), g))
    Gr = {}
    for c, d in items:
        g = dirs[d][4][0, c] + gbc_ref[...]
        Gr[c, d] = jnp.where(f_row, _log_sigmoid(g), g)
    Gi = [pltpu.roll(Gc[d], C_HEADS, axis=1) for d in range(2)]
    Gri = {i: jnp.roll(Gr[i], C_HEADS, axis=0) for i in items}
    Fc = {(c, d): _dot_exact_lhs(tri_col[d], Gc[d][csl[c]]) for c, d in items}
    Fr = {(c, d): _dot_exact_rhs(Gr[c, d], tri_row[d]) for c, d in items}
    FLl = {(c, d): Fc[c, d][L - 1:L] if d == 0 else Fc[c, d][0:1] for c, d in items}
    WL = {(c, d): FLl[c, d] - Fc[c, d] + Gi[d][csl[c]] for c, d in items}
    wl_max = {i: jnp.max(WL[i], axis=0, keepdims=True) for i in items}
    w_r = {i: Gri[i] - Fr[i] for i in items}
    neg = jnp.full((L, GATE_PAD), -jnp.inf, F32)
    P = {}
    for c, d in items:
        x = Gi[d][csl[c]] - Fc[c, d]
        sh = 1
        while sh < L:
            shifted = (jnp.concatenate([neg[:sh], x[:L - sh]], axis=0) if d == 0
                       else jnp.concatenate([x[sh:], neg[:sh]], axis=0))
            x = jnp.maximum(x, shifted)
            sh *= 2
        P[c, d] = x
    m_old_l, m_new_l = {}, {}
    ml = [m_scr[d] for d in range(2)]
    for ci in range(nchunk):
        for d in range(2):
            c = ci if d == 0 else nchunk - 1 - ci
            m_old_l[c, d] = ml[d]
            ml[d] = jnp.maximum(FLl[c, d] + ml[d], wl_max[c, d])
            m_new_l[c, d] = ml[d]
    MT = {i: Fc[i] + jnp.maximum(m_old_l[i], P[i]) for i in items}
    U = {i: Fc[i] - MT[i] for i in items}
    INTER = {i: jnp.exp(Fc[i] + m_old_l[i] - MT[i]) for i in items}
    EMT = {i: jnp.exp(-MT[i]) for i in items}
    WS = {i: jnp.exp(WL[i] - m_new_l[i]) for i in items}
    DEC = {i: jnp.exp(FLl[i] + m_old_l[i] - m_new_l[i]) for i in items}
    na = dv + dk
    col = lambda X, k_, n: jnp.broadcast_to(X[k_[0], k_[1]][:, fi[k_[1], k_[2]]:fi[k_[1], k_[2]] + 1], (L, n))
    qc = {(c, d, h): dirs[d][0][0, csl[c], h * dk:(h + 1) * dk] for c, d, h in chains}
    kf = {(c, d, h): dirs[d][1][0, csl[c], h * dk:(h + 1) * dk].astype(F32) * (dk ** -0.5) for c, d, h in chains}
    ones = jnp.ones((L, dk), BF16)
    va = {(c, d, h): jnp.concatenate([dirs[d][2][0, csl[c], h * dv:(h + 1) * dv], ones], axis=1)
          for c, d, h in chains}
    qk = {k_: lax.dot_general(qc[k_], kf[k_].astype(BF16), _NT, preferred_element_type=F32) for k_ in chains}
    W = {k_: (jnp.exp(jnp.where(keeps[k_[1]], col(U, k_, L) + w_r[k_[0], k_[1]][fi[k_[1], k_[2]]:fi[k_[1], k_[2]] + 1],
                                -jnp.inf)) * qk[k_]).astype(BF16) for k_ in chains}
    Wv = {k_: jnp.dot(W[k_], va[k_], preferred_element_type=F32) for k_ in chains}
    kw = {k_: (kf[k_] * col(WS, k_, dk)).astype(BF16) for k_ in chains}
    kv = {k_: lax.dot_general(kw[k_], va[k_], _TN, preferred_element_type=F32) for k_ in chains}
    inter = {k_: col(INTER, k_, dk) for k_ in chains}
    e_mt = {k_: col(EMT, k_, dk) for k_ in chains}
    dec = {(c, d, h): DEC[c, d][:, fi[d, h]:fi[d, h] + 1] for c, d, h in chains}
    Cs = {(d, h): c_scr[d, h] for d in range(2) for h in range(C_HEADS)}
    for ci in range(nchunk):
        for d in range(2):
            c = ci if d == 0 else nchunk - 1 - ci
            for h in range(C_HEADS):
                k_ = (c, d, h)
                C = Cs[d, h]
                g = inter[k_]
                num = (jnp.concatenate([g] * (na // dk), axis=1)
                       * jnp.dot(qc[k_], C.astype(BF16), preferred_element_type=F32) + Wv[k_])
                den = jnp.maximum(jnp.abs(num[:, dv:]), e_mt[k_])
                dirs[d][5][0, csl[c], h * dv:(h + 1) * dv] = num[:, :dv] / jnp.concatenate([den] * (dv // dk), axis=1)
                Cs[d, h] = dec[k_] * C + kv[k_]
    for d in range(2):
        for h in range(C_HEADS):
            c_scr[d, h] = Cs[d, h]
        m_scr[d] = ml[d]


def _mlstm(z, gcol, grow, gate_b, *, tg=512):
    B, T, _ = z.shape
    nb = T // tg
    qw = C_HEADS * C_QK_DIM
    vw = C_HEADS * C_V_DIM
    ng = grow.shape[2]
    fwd = lambda b, i: (b, i)
    bwd = lambda b, i: (b, nb - 1 - i)

    def specs(order):
        return [
            pl.BlockSpec((1, tg, qw), lambda b, i: (*order(b, i), 0)),
            pl.BlockSpec((1, tg, qw), lambda b, i: (*order(b, i), 1)),
            pl.BlockSpec((1, tg, vw), lambda b, i: (*order(b, i), 1)),
            pl.BlockSpec((1, tg, GATE_PAD), lambda b, i: (*order(b, i), 0)),
            pl.BlockSpec((1, tg // C_CHUNK, ng, C_CHUNK), lambda b, i: (*order(b, i), 0, 0)),
        ]

    kern = functools.partial(_mlstm_kernel, tg=tg)
    return pl.pallas_call(
        kern,
        grid=(B, nb),
        in_specs=[pl.BlockSpec((1, GATE_PAD), lambda b, i: (0, 0)),
                  pl.BlockSpec((ng, 1), lambda b, i: (0, 0))] + specs(fwd) + specs(bwd),
        out_specs=[
            pl.BlockSpec((1, tg, vw), lambda b, i: (b, i, 0)),
            pl.BlockSpec((1, tg, vw), lambda b, i: (b, nb - 1 - i, 0)),
        ],
        out_shape=[jax.ShapeDtypeStruct((B, T, vw), F32)] * 2,
        scratch_shapes=[
            pltpu.VMEM((2, C_HEADS, C_QK_DIM, C_V_DIM + C_QK_DIM), F32),
            pltpu.VMEM((2, 1, GATE_PAD), F32),
        ],
        compiler_params=_params(("parallel", "arbitrary")),
        name="mlstm",
    )(_pad_cols(gate_b[None, :], GATE_PAD), gate_b[:, None], z, z, z, gcol, grow, z, z, z, gcol, grow)


def _pad_cols(w, n):
    return jnp.pad(w, ((0, 0), (0, n - w.shape[1])))


def _trunk(x, mod, p):
    for i in range(DEPTH):
        sh1, sc1, g1, sh2, sc2, g2 = [m[:, None, :] for m in jnp.split(mod[i], 6, axis=-1)]
        if i % 2 == 0:
            j = i // 2
            lam_init = 0.8 - 0.6 * math.exp(-0.3 * i)
            z, zlr, vt = _inproj(x, sh1, sc1, p["norm1_g"][i], p["even_w_main"][j], p["even_w_lr"][j],
                                 p["even_w_vt"][j], _alibi_key_features(), with_rows=False)
            a = _diff_attention(z, vt, p["even_lam"][j], p["even_sub_g"][j], lam_init)
            of, ob = _gla(z, zlr, p["even_gk_w"][j], p["even_gk_b"][j])
            mixer, gate_blk, gn, w_out = (a, of, ob), EVEN_BG // a.shape[-1], p["even_gla_norm_g"][j], p["even_w_out"][j]
        else:
            j = i // 2
            z, gcol, grow = _inproj(x, sh1, sc1, p["norm1_g"][i], p["odd_w_main"][j], p["odd_w_gate"][j],
                                    p["odd_w_gate_t"][j], with_rows=True, chunk=C_CHUNK)
            hf, hb = _mlstm(z, gcol, grow, p["odd_gate_b"][j])
            mixer, gate_blk, gn, w_out = (hf, hb), ODD_CO // hf.shape[-1], p["odd_norm_g"][j], p["odd_w_out"][j]
        x = _mix_ffn(mixer, z, gate_blk, gn, x, g1, w_out, sh2, sc2, p["norm2_g"][i], g2, p["ffn_w_gu"][i],
                     p["ffn_w_down"][i], p["final_g"], even=(i % 2 == 0), final_norm=(i == DEPTH - 1))
    return x


def kernel(x_prompt, x_sample, c_prompt, c_sample, w_mod, b_mod, norm1_g, norm2_g, even_w_in, even_lam_q1, even_lam_k1, even_lam_q2, even_lam_k2, even_attn_sub_g, even_gk_w_f, even_gk_b_f, even_gk_w_b, even_gk_b_b, even_gla_norm_g, even_w_out, odd_w_in, odd_gate_b, odd_norm_g, odd_w_out, ffn_w_gu, ffn_w_down, final_g):
    n_even = even_w_in.shape[0]
    n_odd = odd_w_in.shape[0]
    r = B_GATE_RANK
    kw = B_HEADS * B_K_DIM
    gk_w = jnp.zeros((n_even, 2, GATE_PAD, kw), F32)
    gk_w = gk_w.at[:, 0, 0:r].set(even_gk_w_f).at[:, 1, r:2 * r].set(even_gk_w_b)
    p = {
        "norm1_g": norm1_g[:, None, :],
        "norm2_g": norm2_g[:, None, :],
        "final_g": final_g[None, :],
        "even_w_main": jnp.concatenate([even_w_in[:, :, :EVEN_AV], even_w_in[:, :, EVEN_REST:EVEN_MAIN]],
                                       axis=-1).astype(BF16),
        "even_w_vt": jnp.swapaxes(even_w_in[:, :, EVEN_AV:EVEN_REST], 1, 2).astype(BF16),
        "even_w_lr": jnp.stack([_pad_cols(even_w_in[j, :, EVEN_MAIN:], GATE_PAD) for j in range(n_even)]),
        "even_lam": jnp.stack([even_lam_q1, even_lam_k1, even_lam_q2, even_lam_k2], axis=1),
        "even_sub_g": even_attn_sub_g[:, :, None],
        "even_gk_w": gk_w,
        "even_gk_b": jnp.stack([even_gk_b_f, even_gk_b_b], axis=1)[:, :, None, :],
        "even_gla_norm_g": even_gla_norm_g[:, None, :],
        "even_w_out": even_w_out.astype(BF16),
        "odd_w_main": odd_w_in[:, :, :ODD_MAIN].astype(BF16),
        "odd_w_gate": jnp.stack([_pad_cols(odd_w_in[j, :, ODD_MAIN:], GATE_PAD) for j in range(n_odd)]),
        "odd_w_gate_t": jnp.swapaxes(odd_w_in[:, :, ODD_MAIN:], 1, 2),
        "odd_gate_b": odd_gate_b,
        "odd_norm_g": odd_norm_g[:, None, :],
        "odd_w_out": odd_w_out.astype(BF16),
        "ffn_w_gu": ffn_w_gu.astype(BF16),
        "ffn_w_down": ffn_w_down.astype(BF16),
    }
    nbp = c_prompt.shape[0]
    mod = _modulation(jnp.concatenate([c_prompt, c_sample], axis=0), w_mod, b_mod)
    y_prompt = _trunk(x_prompt, mod[:, :nbp], p)
    y_sample = _trunk(x_sample, mod[:, nbp:], p)
    return (y_prompt, y_sample)
```

```python
import functools
import math

import jax
import jax.numpy as jnp
from jax import lax
from jax.experimental import pallas as pl
from jax.experimental.pallas import tpu as pltpu

F32 = jnp.float32
BF16 = jnp.bfloat16
HIGHEST = lax.Precision.HIGHEST

D_MODEL = 1024
DEPTH = 2
EPS = 1e-6
A_HEADS = 4
A_QK_DIM = 64
A_V_DIM = 128
B_HEADS = 4
B_K_DIM = 64
B_V_DIM = 128
B_GATE_RANK = 16
B_GATE_NORM = 16.0
B_CHUNK = 64
C_HEADS = 4
C_QK_DIM = 128
C_V_DIM = 256
C_CHUNK = 64
D_FF = 2816
EVEN_AV = 1024
EVEN_REST = 1536
EVEN_MAIN = 3072
EVEN_AK = 512
EVEN_BQ = 1536
EVEN_BV = 2048
EVEN_BG = 2560
ODD_MAIN = 3072
ODD_CO = 2048
GATE_PAD = 128
ONES_ROWS = 16
LOG2E = 1.4426950408889634
ATTN_SUB = 256
ATTN_AHEAD = 1
ATTN_STREAMS = 8

VMEM_LIMIT = 56 * 1024 * 1024

_NT = (((1,), (1,)), ((), ()))
_TN = (((0,), (0,)), ((), ()))


def _log_sigmoid(x):
    return jnp.minimum(x, 0.0) - jnp.log1p(jnp.exp(-jnp.abs(x)))


def _silu(x):
    return x * jax.nn.sigmoid(x)


def _split_bf16(x, terms):
    parts = []
    for _ in range(terms - 1):
        hi = x.astype(BF16)
        parts.append(hi)
        x = x - hi.astype(F32)
    parts.append(x.astype(BF16))
    return parts


SPLIT_TERMS = 2


def _dot_exact_lhs(a_bf16, x):
    return sum(jnp.dot(a_bf16, part, preferred_element_type=F32) for part in _split_bf16(x, SPLIT_TERMS))


def _dot_exact_rhs(x, a_bf16):
    return sum(jnp.dot(part, a_bf16, preferred_element_type=F32) for part in _split_bf16(x, SPLIT_TERMS))


def _params(sem):
    return pltpu.CompilerParams(dimension_semantics=sem, vmem_limit_bytes=VMEM_LIMIT)


def _mod_kernel(c_ref, w_ref, b_ref, o_ref):
    s = _silu(c_ref[...])
    o_ref[0] = jnp.dot(s, w_ref[0], precision=HIGHEST, preferred_element_type=F32) + b_ref[0]


def _modulation(c_all, w_mod, b_mod):
    nb = c_all.shape[0]
    tn = 1536
    return pl.pallas_call(
        _mod_kernel,
        grid=(DEPTH, 6 * D_MODEL // tn),
        in_specs=[
            pl.BlockSpec((nb, D_MODEL), lambda i, j: (0, 0)),
            pl.BlockSpec((1, D_MODEL, tn), lambda i, j: (i, 0, j)),
            pl.BlockSpec((1, 1, tn), lambda i, j: (i, 0, j)),
        ],
        out_specs=pl.BlockSpec((1, nb, tn), lambda i, j: (i, 0, j)),
        out_shape=jax.ShapeDtypeStruct((DEPTH, nb, 6 * D_MODEL), F32),
        compiler_params=_params(("parallel", "parallel")),
        name="modulation",
    )(c_all, w_mod, b_mod.reshape(DEPTH, 1, 6 * D_MODEL))


def _norm_mod(x, g, sh, sc):
    ms = jnp.mean(x * x, axis=-1, keepdims=True)
    y = x * lax.rsqrt(ms + EPS) * g
    return y * (1.0 + sc) + sh


def _inproj_kernel(x_ref, sh_ref, sc_ref, g_ref, w_ref, wg_ref, wt_ref, *rest, n_main, n_col, chunk, with_rows,
                   row_tile):
    if with_rows:
        z_ref, zg_ref, zt_ref = rest
    else:
        kf_ref, z_ref, zg_ref, zt_ref = rest
    for r in range(x_ref.shape[1] // row_tile):
        rs = slice(r * row_tile, (r + 1) * row_tile)
        hb = _norm_mod(x_ref[0, rs, :], g_ref[...], sh_ref[0], sc_ref[0]).astype(BF16)
        for j in range(n_main // n_col):
            zc = jnp.dot(hb, w_ref[:, j * n_col:(j + 1) * n_col], preferred_element_type=F32)
            if with_rows or j == 0:
                z_ref[0, rs, j * n_col:(j + 1) * n_col] = zc.astype(BF16)
            elif j == 1:
                hw = 2 * A_QK_DIM
                dup = jnp.concatenate([zc[:, (i // 2) * hw:(i // 2 + 1) * hw] for i in range(2 * A_HEADS)], axis=1)
                pos = (lax.broadcasted_iota(jnp.int32, (row_tile, 1), 0) & (ATTN_SUB - 1)).astype(F32)
                feat = kf_ref[1:2, :] + kf_ref[2:3, :] * pos
                z_ref[0, rs, n_col:3 * n_col] = jnp.where(kf_ref[0:1, :] > 0.0, dup, feat).astype(BF16)
            else:
                z_ref[0, rs, (j + 1) * n_col:(j + 2) * n_col] = zc.astype(BF16)
        zg_ref[0, rs, :] = jnp.dot(hb, wg_ref[...].astype(BF16), preferred_element_type=F32)
        if with_rows:
            for c in range(row_tile // chunk):
                zt_ref[0, r * (row_tile // chunk) + c] = lax.dot_general(
                    wt_ref[...].astype(BF16), hb[c * chunk:(c + 1) * chunk], _NT, preferred_element_type=F32)
        else:
            for c in range(row_tile // ATTN_SUB):
                zt_ref[0, r * (row_tile // ATTN_SUB) + c] = lax.dot_general(
                    wt_ref[...], hb[c * ATTN_SUB:(c + 1) * ATTN_SUB], _NT,
                    preferred_element_type=F32).astype(BF16)


def _inproj(x, sh, sc, g, w_main, w_gate, w_t, k_feat=None, *, with_rows, tm=1024, row_tile=512, chunk=64):
    B, T, _ = x.shape
    n_main = w_main.shape[1]
    n_col = 512
    n_z = n_main if with_rows else n_main + n_col
    kern = functools.partial(_inproj_kernel, n_main=n_main, n_col=n_col, chunk=chunk, with_rows=with_rows,
                             row_tile=row_tile)
    in_specs = [
        pl.BlockSpec((1, tm, D_MODEL), lambda b, i: (b, i, 0)),
        pl.BlockSpec((1, 1, D_MODEL), lambda b, i: (b, 0, 0)),
        pl.BlockSpec((1, 1, D_MODEL), lambda b, i: (b, 0, 0)),
        pl.BlockSpec((1, D_MODEL), lambda b, i: (0, 0)),
        pl.BlockSpec((D_MODEL, n_main), lambda b, i: (0, 0)),
        pl.BlockSpec((D_MODEL, GATE_PAD), lambda b, i: (0, 0)),
    ]
    out_specs = [
        pl.BlockSpec((1, tm, n_z), lambda b, i: (b, i, 0)),
        pl.BlockSpec((1, tm, GATE_PAD), lambda b, i: (b, i, 0)),
    ]
    out_shape = [
        jax.ShapeDtypeStruct((B, T, n_z), BF16),
        jax.ShapeDtypeStruct((B, T, GATE_PAD), F32),
    ]
    args = [x, sh, sc, g, w_main, w_gate, w_t]
    nt = w_t.shape[0]
    in_specs.append(pl.BlockSpec((nt, D_MODEL), lambda b, i: (0, 0)))
    if with_rows:
        out_specs.append(pl.BlockSpec((1, tm // chunk, nt, chunk), lambda b, i: (b, i, 0, 0)))
        out_shape.append(jax.ShapeDtypeStruct((B, T // chunk, nt, chunk), F32))
    else:
        args.append(k_feat)
        in_specs.append(pl.BlockSpec(k_feat.shape, lambda b, i: (0, 0)))
        out_specs.append(pl.BlockSpec((1, tm // ATTN_SUB, nt, ATTN_SUB), lambda b, i: (b, i, 0, 0)))
        out_shape.append(jax.ShapeDtypeStruct((B, T // ATTN_SUB, nt, ATTN_SUB), BF16))
    return pl.pallas_call(
        kern,
        grid=(B, T // tm),
        in_specs=in_specs,
        out_specs=out_specs,
        out_shape=out_shape,
        compiler_params=_params(("parallel", "parallel")),
        name="inproj_rows" if with_rows else "inproj",
    )(*args)


def _attn_kernel(slope_ref, q_ref, ka_ref, kb_ref, vt_ref, lamp_ref, subg_ref, o_ref, q_scr, *, n_sub, lam_init):
    h = pl.program_id(1)
    tq = sk = ATTN_SUB
    dv = A_V_DIM
    hw = 2 * A_QK_DIM
    slope2 = slope_ref[h, 0] + slope_ref[h, 1] + slope_ref[h, 2]
    streams = range(ATTN_STREAMS)
    qis = [pl.program_id(2) * ATTN_STREAMS + st for st in streams]

    lane = lax.broadcasted_iota(jnp.int32, (1, hw), 1)
    c_off = lax.broadcasted_iota(jnp.int32, (tq, hw), 0).astype(F32)
    for c in range(2):
        own = (lane < A_QK_DIM) if c == 0 else (lane >= A_QK_DIM)
        base = (1 - c) * A_QK_DIM
        q_feat = jnp.zeros((tq, hw), F32)
        for x in range(3):
            q_feat = jnp.where(lane == base + x, slope_ref[h, x], q_feat)
            q_feat = jnp.where(lane == base + 3 + x, -c_off, q_feat)
        for st in streams:
            qs = (q_ref[0, st * tq:(st + 1) * tq, :].astype(F32) * (A_QK_DIM ** -0.5 * LOG2E)).astype(BF16)
            for v, sign in enumerate((1.0, -1.0, 0.0)):
                q_scr[st, v, c] = jnp.where(own, qs, (sign * q_feat).astype(BF16))
    rel = (lax.broadcasted_iota(jnp.int32, (sk, tq), 0)
           - lax.broadcasted_iota(jnp.int32, (sk, tq), 1)).astype(F32)
    diag_bias = -slope2 * jnp.abs(rel)
    ones = jnp.ones((ONES_ROWS, sk), BF16)
    k_refs = (ka_ref, kb_ref)

    def scores(st, j):
        qi = qis[st]
        if j == 0:
            idx, const, v = qi, jnp.zeros((), F32), 2
        else:
            idx = qi + j
            idx = jnp.where(idx >= n_sub, idx - n_sub, idx)
            before = idx < qi
            const = jnp.where(before, 1.0, -1.0) * slope2 * ((idx - qi) * sk).astype(F32)
            v = jnp.where(before, 0, 1)
        rows = pl.ds(pl.multiple_of(idx * sk, sk), sk)
        s = [lax.dot_general(k_refs[c][0, rows, :], q_scr[st, v, c], _NT, preferred_element_type=F32)
             for c in range(2)]
        if j == 0:
            s = [s_c + diag_bias for s_c in s]
        s = jnp.concatenate(s, axis=1)
        return s, jnp.max(s, axis=0, keepdims=True) + const, const, idx

    def accumulate(acc, alpha, p, idx):
        vt1 = jnp.concatenate([vt_ref[0, idx], ones], axis=0)
        return alpha * acc + jnp.dot(vt1, p, preferred_element_type=F32)

    m = [jnp.full((1, 2 * tq), -jnp.inf, F32) for _ in streams]
    acc = [jnp.zeros((dv + ONES_ROWS, 2 * tq), F32) for _ in streams]
    ahead = [[scores(st, j) for j in range(min(ATTN_AHEAD, n_sub))] for st in streams]
    pending = [None for _ in streams]
    for j in range(n_sub):
        cur = [ahead[st].pop(0) for st in streams]
        if j + ATTN_AHEAD < n_sub:
            for st in streams:
                ahead[st].append(scores(st, j + ATTN_AHEAD))
        for st in streams:
            s, m_loc, const, idx = cur[st]
            m_new = jnp.maximum(m[st], m_loc)
            alpha = jnp.exp2(m[st] - m_new)
            p = jnp.exp2(s - (m_new - const)).astype(BF16)
            m[st] = m_new
            cur[st] = (alpha, p, idx)
        for st in streams:
            if pending[st] is not None:
                acc[st] = accumulate(acc[st], *pending[st])
            pending[st] = cur[st]
    lp = lamp_ref[...]
    lam = (jnp.exp(jnp.sum(lp[0:1] * lp[1:2], axis=-1, keepdims=True))
           - jnp.exp(jnp.sum(lp[2:3] * lp[3:4], axis=-1, keepdims=True)) + lam_init)
    for st in streams:
        a = accumulate(acc[st], *pending[st])
        on = a[:dv] / a[dv:dv + 1]
        o = on[:, :tq] - lam * on[:, tq:]
        ms = jnp.mean(o * o, axis=0, keepdims=True)
        y = o * lax.rsqrt(ms + EPS) * subg_ref[...] * (1.0 - lam_init)
        o_ref[0, st * tq:(st + 1) * tq, :] = y.T.astype(o_ref.dtype)


def _alibi_slope_terms():
    slope2 = jnp.exp2(-8.0 * jnp.arange(1, A_HEADS + 1, dtype=F32) / A_HEADS) * LOG2E
    s0 = slope2.astype(BF16).astype(F32)
    s1 = (slope2 - s0).astype(BF16).astype(F32)
    s2 = (slope2 - s0 - s1).astype(BF16).astype(F32)
    return jnp.stack([s0, s1, s2], axis=1)


def _alibi_key_features():
    hw = 2 * A_QK_DIM
    terms = _alibi_slope_terms()
    lane = jnp.arange(hw)
    rows = []
    for h in range(A_HEADS):
        for c in range(2):
            base = (1 - c) * A_QK_DIM
            keep = ((lane < A_QK_DIM) if c == 0 else (lane >= A_QK_DIM)).astype(F32)
            const = jnp.zeros((hw,), F32).at[base + 3:base + 6].set(terms[h])
            pos = jnp.zeros((hw,), F32).at[base:base + 3].set(1.0)
            rows.append(jnp.stack([keep, const, pos]))
    return jnp.concatenate(rows, axis=1)


def _diff_attention(z, vt, lam_params, sub_g_col, lam_init):
    B, T, _ = z.shape
    n_sub = T // ATTN_SUB
    assert T % (ATTN_SUB * ATTN_STREAMS) == 0, T
    slopes = _alibi_slope_terms()
    kern = functools.partial(_attn_kernel, n_sub=n_sub, lam_init=lam_init)
    hw = 2 * A_QK_DIM
    return pl.pallas_call(
        kern,
        grid=(B, A_HEADS, n_sub // ATTN_STREAMS),
        in_specs=[
            pl.BlockSpec(memory_space=pltpu.SMEM),
            pl.BlockSpec((1, ATTN_STREAMS * ATTN_SUB, hw), lambda b, h, i: (b, i, h)),
            pl.BlockSpec((1, T, hw), lambda b, h, i: (b, 0, EVEN_AK // hw + 2 * h)),
            pl.BlockSpec((1, T, hw), lambda b, h, i: (b, 0, EVEN_AK // hw + 2 * h + 1)),
            pl.BlockSpec((1, n_sub, A_V_DIM, ATTN_SUB), lambda b, h, i: (b, 0, h, 0)),
            pl.BlockSpec((4, A_QK_DIM), lambda b, h, i: (0, 0)),
            pl.BlockSpec((A_V_DIM, 1), lambda b, h, i: (0, 0)),
        ],
        out_specs=pl.BlockSpec((1, ATTN_STREAMS * ATTN_SUB, A_V_DIM), lambda b, h, i: (b, i, h)),
        out_shape=jax.ShapeDtypeStruct((B, T, A_HEADS * A_V_DIM), BF16),
        scratch_shapes=[pltpu.VMEM((ATTN_STREAMS, 3, 2, ATTN_SUB, hw), BF16)],
        compiler_params=_params(("parallel", "parallel", "parallel")),
        name="diff_attention",
    )(slopes, z, z, z, vt, lam_params, sub_g_col)


def _gla_kernel(qf_ref, kf_ref, vf_ref, lrf_ref, qb_ref, kb_ref, vb_ref, lrb_ref, gw_ref, gb_ref,
                of_ref, ob_ref, s_scr, *, tg):
    L = B_CHUNK
    nchunk = tg // L
    pw = 2 * B_K_DIM
    vw = 2 * B_V_DIM

    @pl.when(pl.program_id(1) == 0)
    def _():
        s_scr[...] = jnp.zeros(s_scr.shape, F32)

    tt = lax.broadcasted_iota(jnp.int32, (L, L), 0)
    ss = lax.broadcasted_iota(jnp.int32, (L, L), 1)
    lane = lax.broadcasted_iota(jnp.int32, (1, pw), 1)
    srow = lax.broadcasted_iota(jnp.int32, (vw, pw), 0)
    slane = lax.broadcasted_iota(jnp.int32, (vw, pw), 1)
    same_head = (srow >= B_V_DIM) == (slane >= B_K_DIM)
    zero_b = jnp.zeros((), BF16)

    dirs = ((qf_ref, kf_ref, vf_ref, lrf_ref, of_ref), (qb_ref, kb_ref, vb_ref, lrb_ref, ob_ref))
    keeps = (ss <= tt, ss >= tt)
    tris = [kp.astype(BF16) for kp in keeps]
    lgs = []
    for d in range(2):
        pre = jnp.dot(dirs[d][3][0].astype(BF16), gw_ref[d].astype(BF16), preferred_element_type=F32) + gb_ref[d]
        lgs.append(_log_sigmoid(pre) * (1.0 / B_GATE_NORM))
    np_ = B_HEADS // 2
    states = [[s_scr[d, p] for p in range(np_)] for d in range(2)]
    sels = (lane < B_K_DIM, lane >= B_K_DIM)
    csl = [slice(c * L, (c + 1) * L) for c in range(nchunk)]
    ksl = [slice(p * pw, (p + 1) * pw) for p in range(np_)]
    items = [(c, d) for c in range(nchunk) for d in range(2)]
    chains = [(c, d, p) for c, d in items for p in range(np_)]
    bs = {(c, d): _dot_exact_lhs(tris[d], lgs[d][csl[c]]) for c, d in items}
    b_ends = {(c, d): bs[c, d][L - 1:L] if d == 0 else bs[c, d][0:1] for c, d in items}
    qcs = {(c, d): dirs[d][0][0, csl[c], :].astype(F32) for c, d in items}
    kcs = {(c, d): dirs[d][1][0, csl[c], :].astype(F32) for c, d in items}
    qes = {i: (qcs[i] * (B_K_DIM ** -0.5) * jnp.exp(bs[i])).astype(BF16) for i in items}
    kes = {i: (kcs[i] * jnp.exp(-bs[i])).astype(BF16) for i in items}
    kds = {i: (kcs[i] * jnp.exp(b_ends[i] - bs[i])).astype(BF16) for i in items}
    decs = {i: jnp.exp(b_ends[i]) for i in items}
    v_ps = {(c, d, p): dirs[d][2][0, csl[c], p * vw:(p + 1) * vw] for c, d, p in chains}
    atts = {(c, d, p, hh): lax.dot_general(jnp.where(sels[hh], qes[c, d][:, ksl[p]], zero_b), kes[c, d][:, ksl[p]],
                                           _NT, preferred_element_type=F32)
            for c, d, p in chains for hh in range(2)}
    atts = {key: jnp.where(keeps[key[1]], a, 0.0).astype(BF16) for key, a in atts.items()}
    intras = {(c, d, p): jnp.concatenate(
        [jnp.dot(atts[c, d, p, hh], v_ps[c, d, p][:, hh * B_V_DIM:(hh + 1) * B_V_DIM],
                 preferred_element_type=F32) for hh in range(2)], axis=-1) for c, d, p in chains}
    uts = {(c, d, p): jnp.where(same_head, lax.dot_general(v_ps[c, d, p], kds[c, d][:, ksl[p]], _TN,
                                                           preferred_element_type=F32), 0.0)
           for c, d, p in chains}
    for ci in range(nchunk):
        for d in range(2):
            c = ci if d == 0 else nchunk - 1 - ci
            for p in range(np_):
                st = states[d][p]
                inter = lax.dot_general(qes[c, d][:, ksl[p]], st.astype(BF16), _NT, preferred_element_type=F32)
                dirs[d][4][0, csl[c], p * vw:(p + 1) * vw] = inter + intras[c, d, p]
                states[d][p] = decs[c, d][:, ksl[p]] * st + uts[c, d, p]
    for d in range(2):
        for p in range(np_):
            s_scr[d, p] = states[d][p]


def _gla(z, zlr, gw, gb, *, tg=1024):
    B, T, _ = z.shape
    nb = T // tg
    kw = B_HEADS * B_K_DIM
    vw = B_HEADS * B_V_DIM
    q_blk = EVEN_BQ // kw
    v_blk = EVEN_BV // vw
    fwd = lambda b, i: (b, i)
    bwd = lambda b, i: (b, nb - 1 - i)

    def specs(order):
        return [
            pl.BlockSpec((1, tg, kw), lambda b, i: (*order(b, i), q_blk)),
            pl.BlockSpec((1, tg, kw), lambda b, i: (*order(b, i), q_blk + 1)),
            pl.BlockSpec((1, tg, vw), lambda b, i: (*order(b, i), v_blk)),
            pl.BlockSpec((1, tg, GATE_PAD), lambda b, i: (*order(b, i), 0)),
        ]

    kern = functools.partial(_gla_kernel, tg=tg)
    return pl.pallas_call(
        kern,
        grid=(B, nb),
        in_specs=specs(fwd) + specs(bwd) + [
            pl.BlockSpec((2, GATE_PAD, kw), lambda b, i: (0, 0, 0)),
            pl.BlockSpec((2, 1, kw), lambda b, i: (0, 0, 0)),
        ],
        out_specs=[
            pl.BlockSpec((1, tg, vw), lambda b, i: (b, i, 0)),
            pl.BlockSpec((1, tg, vw), lambda b, i: (b, nb - 1 - i, 0)),
        ],
        out_shape=[jax.ShapeDtypeStruct((B, T, vw), F32)] * 2,
        scratch_shapes=[pltpu.VMEM((2, B_HEADS // 2, 2 * B_V_DIM, 2 * B_K_DIM), F32)],
        compiler_params=_params(("parallel", "arbitrary")),
        name="gla",
    )(z, z, z, zlr, z, z, z, zlr, gw, gb)


def _head_rmsnorm(o, g, width):
    parts = []
    for h in range(o.shape[-1] // width):
        oh = o[:, h * width:(h + 1) * width]
        ms = jnp.mean(oh * oh, axis=-1, keepdims=True)
        parts.append(oh * lax.rsqrt(ms + EPS) * g[:, h * width:(h + 1) * width])
    return jnp.concatenate(parts, axis=-1)


def _mix_ffn_kernel(*refs, even, final_norm, ff_chunk, row_tile):
    if even:
        a_ref, of_ref, ob_ref, gate_ref = refs[:4]
        refs = refs[4:]
    else:
        hf_ref, hb_ref, gate_ref = refs[:3]
        refs = refs[3:]
    gn_ref, x_ref, g1_ref, wo_ref, sh_ref, sc_ref, g_ref, g2_ref, wgu_ref, wd_ref, fg_ref, o_ref = refs
    tm = x_ref.shape[1]
    for r in range(tm // row_tile):
        rs = slice(r * row_tile, (r + 1) * row_tile)
        if even:
            o = _head_rmsnorm(of_ref[0, rs, :] + ob_ref[0, rs, :], gn_ref[...], B_V_DIM)
            b_out = (o * _silu(gate_ref[0, rs, :].astype(F32))).astype(BF16)
            mix = jnp.concatenate([a_ref[0, rs, :], b_out], axis=-1)
        else:
            ht = _head_rmsnorm(hf_ref[0, rs, :] + hb_ref[0, rs, :], gn_ref[...], C_V_DIM)
            mix = (jax.nn.sigmoid(gate_ref[0, rs, :].astype(F32)) * ht).astype(BF16)
        x1 = x_ref[0, rs, :] + g1_ref[0] * jnp.dot(mix, wo_ref[...], preferred_element_type=F32)
        hb = _norm_mod(x1, g_ref[...], sh_ref[0], sc_ref[0]).astype(BF16)
        acts = []
        for c in range(D_FF // ff_chunk):
            gate = jnp.dot(hb, wgu_ref[:, c * ff_chunk:(c + 1) * ff_chunk], preferred_element_type=F32)
            up = jnp.dot(hb, wgu_ref[:, D_FF + c * ff_chunk:D_FF + (c + 1) * ff_chunk],
                         preferred_element_type=F32)
            acts.append((_silu(gate) * up).astype(BF16))
        act = jnp.concatenate(acts, axis=-1)
        y = x1 + g2_ref[0] * jnp.dot(act, wd_ref[...], preferred_element_type=F32)
        if final_norm:
            ms = jnp.mean(y * y, axis=-1, keepdims=True)
            y = y * lax.rsqrt(ms + EPS) * fg_ref[...]
        o_ref[0, rs, :] = y


def _mix_ffn(mixer_outs, z, gate_blk, gn, x, g1, w_out, sh, sc, g, g2, w_gu, w_down, final_g,
             *, even, final_norm, tm=512, ff_chunk=256, row_tile=512):
    B, T, _ = x.shape
    tok = lambda b, i: (b, i, 0)
    per_b = lambda b, i: (b, 0, 0)
    const = lambda b, i: (0, 0)
    resident = functools.partial(pl.BlockSpec, index_map=const, pipeline_mode=pl.Buffered(1))
    nm = mixer_outs[0].shape[-1]
    in_specs = [pl.BlockSpec((1, tm, nm), tok) for _ in mixer_outs]
    in_specs += [
        pl.BlockSpec((1, tm, nm), lambda b, i: (b, i, gate_blk)),
        pl.BlockSpec((1, nm), const),
        pl.BlockSpec((1, tm, D_MODEL), tok),
        pl.BlockSpec((1, 1, D_MODEL), per_b),
        resident(w_out.shape),
        pl.BlockSpec((1, 1, D_MODEL), per_b),
        pl.BlockSpec((1, 1, D_MODEL), per_b),
        pl.BlockSpec((1, D_MODEL), const),
        pl.BlockSpec((1, 1, D_MODEL), per_b),
        resident(w_gu.shape),
        resident(w_down.shape),
        pl.BlockSpec((1, D_MODEL), const),
    ]
    kern = functools.partial(_mix_ffn_kernel, even=even, final_norm=final_norm, ff_chunk=ff_chunk,
                             row_tile=row_tile)
    return pl.pallas_call(
        kern,
        grid=(B, T // tm),
        in_specs=in_specs,
        out_specs=pl.BlockSpec((1, tm, D_MODEL), tok),
        out_shape=jax.ShapeDtypeStruct((B, T, D_MODEL), F32),
        compiler_params=_params(("parallel", "parallel")),
        name=("mix_ffn_even" if even else "mix_ffn_odd") + ("_final" if final_norm else ""),
    )(*mixer_outs, z, gn, x, g1, w_out, sh, sc, g, g2, w_gu, w_down, final_g)


def _mlstm_kernel(gbr_ref, gbc_ref, qf_ref, kf_ref, vf_ref, gcf_ref, grf_ref, qb_ref, kb_ref, vb_ref, gcb_ref,
                  grb_ref, of_ref, ob_ref, c_scr, m_scr, *, tg):
    L = C_CHUNK
    nchunk = tg // L
    dk, dv = C_QK_DIM, C_V_DIM

    @pl.when(pl.program_id(1) == 0)
    def _():
        c_scr[...] = jnp.zeros(c_scr.shape, F32)
        m_scr[...] = jnp.zeros(m_scr.shape, F32)

    tt = lax.broadcasted_iota(jnp.int32, (L, L), 0)
    ss = lax.broadcasted_iota(jnp.int32, (L, L), 1)

    dirs = ((qf_ref, kf_ref, vf_ref, gcf_ref, grf_ref, of_ref),
            (qb_ref, kb_ref, vb_ref, gcb_ref, grb_ref, ob_ref))
    keeps = (ss <= tt, ss >= tt)
    tri_col = [keeps[0].astype(BF16), keeps[1].astype(BF16)]
    tri_row = [keeps[1].astype(BF16), keeps[0].astype(BF16)]
    glane = lax.broadcasted_iota(jnp.int32, (1, GATE_PAD), 1)
    f_lane = jnp.logical_and((glane & C_HEADS) != 0, glane < 4 * C_HEADS)
    f_row = (lax.broadcasted_iota(jnp.int32, (4 * C_HEADS, 1), 0) & C_HEADS) != 0
    csl = [slice(c * L, (c + 1) * L) for c in range(nchunk)]
    items = [(c, d) for c in range(nchunk) for d in range(2)]
    chains = [(c, d, h) for c, d in items for h in range(C_HEADS)]
    fi = {(d, h): (2 * d + 1) * C_HEADS + h for d in range(2) for h in range(C_HEADS)}

    Gc = []
    for d in range(2):
        g = dirs[d][3][0] + gbr_ref[...]
        Gc.append(jnp.where(f_lane, _log_sigmoid(g), g))
    Gr = {}
    for c, d in items:
        g = dirs[d][4][0, c] + gbc_ref[...]
        Gr[c, d] = jnp.where(f_row, _log_sigmoid(g), g)
    Gi = [pltpu.roll(Gc[d], C_HEADS, axis=1) for d in range(2)]
    Gri = {i: jnp.roll(Gr[i], C_HEADS, axis=0) for i in items}
    Fc = {(c, d): _dot_exact_lhs(tri_col[d], Gc[d][csl[c]]) for c, d in items}
    Fr = {(c, d): _dot_exact_rhs(Gr[c, d], tri_row[d]) for c, d in items}
    FLl = {(c, d): Fc[c, d][L - 1:L] if d == 0 else Fc[c, d][0:1] for c, d in items}
    WL = {(c, d): FLl[c, d] - Fc[c, d] + Gi[d][csl[c]] for c, d in items}
    wl_max = {i: jnp.max(WL[i], axis=0, keepdims=True) for i in items}
    w_r = {i: Gri[i] - Fr[i] for i in items}
    neg = jnp.full((L, GATE_PAD), -jnp.inf, F32)
    P = {}
    for c, d in items:
        x = Gi[d][csl[c]] - Fc[c, d]
        sh = 1
        while sh < L:
            shifted = (jnp.concatenate([neg[:sh], x[:L - sh]], axis=0) if d == 0
                       else jnp.concatenate([x[sh:], neg[:sh]], axis=0))
            x = jnp.maximum(x, shifted)
            sh *= 2
        P[c, d] = x
    m_old_l, m_new_l = {}, {}
    ml = [m_scr[d] for d in range(2)]
    for ci in range(nchunk):
        for d in range(2):
            c = ci if d == 0 else nchunk - 1 - ci
            m_old_l[c, d] = ml[d]
            ml[d] = jnp.maximum(FLl[c, d] + ml[d], wl_max[c, d])
            m_new_l[c, d] = ml[d]
    MT = {i: Fc[i] + jnp.maximum(m_old_l[i], P[i]) for i in items}
    U = {i: Fc[i] - MT[i] for i in items}
    INTER = {i: jnp.exp(Fc[i] + m_old_l[i] - MT[i]) for i in items}
    EMT = {i: jnp.exp(-MT[i]) for i in items}
    WS = {i: jnp.exp(WL[i] - m_new_l[i]) for i in items}
    DEC = {i: jnp.exp(FLl[i] + m_old_l[i] - m_new_l[i]) for i in items}
    na = dv + dk
    col = lambda X, k_, n: jnp.broadcast_to(X[k_[0], k_[1]][:, fi[k_[1], k_[2]]:fi[k_[1], k_[2]] + 1], (L, n))
    qc = {(c, d, h): dirs[d][0][0, csl[c], h * dk:(h + 1) * dk] for c, d, h in chains}
    kf = {(c, d, h): dirs[d][1][0, csl[c], h * dk:(h + 1) * dk].astype(F32) * (dk ** -0.5) for c, d, h in chains}
    ones = jnp.ones((L, dk), BF16)
    va = {(c, d, h): jnp.concatenate([dirs[d][2][0, csl[c], h * dv:(h + 1) * dv], ones], axis=1)
          for c, d, h in chains}
    qk = {k_: lax.dot_general(qc[k_], kf[k_].astype(BF16), _NT, preferred_element_type=F32) for k_ in chains}
    W = {k_: (jnp.exp(jnp.where(keeps[k_[1]], col(U, k_, L) + w_r[k_[0], k_[1]][fi[k_[1], k_[2]]:fi[k_[1], k_[2]] + 1],
                                -jnp.inf)) * qk[k_]).astype(BF16) for k_ in chains}
    Wv = {k_: jnp.dot(W[k_], va[k_], preferred_element_type=F32) for k_ in chains}
    kw = {k_: (kf[k_] * col(WS, k_, dk)).astype(BF16) for k_ in chains}
    kv = {k_: lax.dot_general(kw[k_], va[k_], _TN, preferred_element_type=F32) for k_ in chains}
    inter = {k_: col(INTER, k_, dk) for k_ in chains}
    e_mt = {k_: col(EMT, k_, dk) for k_ in chains}
    dec = {(c, d, h): DEC[c, d][:, fi[d, h]:fi[d, h] + 1] for c, d, h in chains}
    Cs = {(d, h): c_scr[d, h] for d in range(2) for h in range(C_HEADS)}
    for ci in range(nchunk):
        for d in range(2):
            c = ci if d == 0 else nchunk - 1 - ci
            for h in range(C_HEADS):
                k_ = (c, d, h)
                C = Cs[d, h]
                g = inter[k_]
                num = (jnp.concatenate([g] * (na // dk), axis=1)
                       * jnp.dot(qc[k_], C.astype(BF16), preferred_element_type=F32) + Wv[k_])
                den = jnp.maximum(jnp.abs(num[:, dv:]), e_mt[k_])
                dirs[d][5][0, csl[c], h * dv:(h + 1) * dv] = num[:, :dv] / jnp.concatenate([den] * (dv // dk), axis=1)
                Cs[d, h] = dec[k_] * C + kv[k_]
    for d in range(2):
        for h in range(C_HEADS):
            c_scr[d, h] = Cs[d, h]
        m_scr[d] = ml[d]


def _mlstm(z, gcol, grow, gate_b, *, tg=512):
    B, T, _ = z.shape
    nb = T // tg
    qw = C_HEADS * C_QK_DIM
    vw = C_HEADS * C_V_DIM
    ng = grow.shape[2]
    fwd = lambda b, i: (b, i)
    bwd = lambda b, i: (b, nb - 1 - i)

    def specs(order):
        return [
            pl.BlockSpec((1, tg, qw), lambda b, i: (*order(b, i), 0)),
            pl.BlockSpec((1, tg, qw), lambda b, i: (*order(b, i), 1)),
            pl.BlockSpec((1, tg, vw), lambda b, i: (*order(b, i), 1)),
            pl.BlockSpec((1, tg, GATE_PAD), lambda b, i: (*order(b, i), 0)),
            pl.BlockSpec((1, tg // C_CHUNK, ng, C_CHUNK), lambda b, i: (*order(b, i), 0, 0)),
        ]

    kern = functools.partial(_mlstm_kernel, tg=tg)
    return pl.pallas_call(
        kern,
        grid=(B, nb),
        in_specs=[pl.BlockSpec((1, GATE_PAD), lambda b, i: (0, 0)),
                  pl.BlockSpec((ng, 1), lambda b, i: (0, 0))] + specs(fwd) + specs(bwd),
        out_specs=[
            pl.BlockSpec((1, tg, vw), lambda b, i: (b, i, 0)),
            pl.BlockSpec((1, tg, vw), lambda b, i: (b, nb - 1 - i, 0)),
        ],
        out_shape=[jax.ShapeDtypeStruct((B, T, vw), F32)] * 2,
        scratch_shapes=[
            pltpu.VMEM((2, C_HEADS, C_QK_DIM, C_V_DIM + C_QK_DIM), F32),
            pltpu.VMEM((2, 1, GATE_PAD), F32),
        ],
        compiler_params=_params(("parallel", "arbitrary")),
        name="mlstm",
    )(_pad_cols(gate_b[None, :], GATE_PAD), gate_b[:, None], z, z, z, gcol, grow, z, z, z, gcol, grow)


def _pad_cols(w, n):
    return jnp.pad(w, ((0, 0), (0, n - w.shape[1])))


def _trunk(x, mod, p):
    for i in range(DEPTH):
        sh1, sc1, g1, sh2, sc2, g2 = [m[:, None, :] for m in jnp.split(mod[i], 6, axis=-1)]
        if i % 2 == 0:
            j = i // 2
            lam_init = 0.8 - 0.6 * math.exp(-0.3 * i)
            z, zlr, vt = _inproj(x, sh1, sc1, p["norm1_g"][i], p["even_w_main"][j], p["even_w_lr"][j],
                                 p["even_w_vt"][j], _alibi_key_features(), with_rows=False)
            a = _diff_attention(z, vt, p["even_lam"][j], p["even_sub_g"][j], lam_init)
            of, ob = _gla(z, zlr, p["even_gk_w"][j], p["even_gk_b"][j])
            mixer, gate_blk, gn, w_out = (a, of, ob), EVEN_BG // a.shape[-1], p["even_gla_norm_g"][j], p["even_w_out"][j]
        else:
            j = i // 2
            z, gcol, grow = _inproj(x, sh1, sc1, p["norm1_g"][i], p["odd_w_main"][j], p["odd_w_gate"][j],
                                    p["odd_w_gate_t"][j], with_rows=True, chunk=C_CHUNK)
            hf, hb = _mlstm(z, gcol, grow, p["odd_gate_b"][j])
            mixer, gate_blk, gn, w_out = (hf, hb), ODD_CO // hf.shape[-1], p["odd_norm_g"][j], p["odd_w_out"][j]
        x = _mix_ffn(mixer, z, gate_blk, gn, x, g1, w_out, sh2, sc2, p["norm2_g"][i], g2, p["ffn_w_gu"][i],
                     p["ffn_w_down"][i], p["final_g"], even=(i % 2 == 0), final_norm=(i == DEPTH - 1))
    return x


def kernel(x_prompt, x_sample, c_prompt, c_sample, w_mod, b_mod, norm1_g, norm2_g, even_w_in, even_lam_q1, even_lam_k1, even_lam_q2, even_lam_k2, even_attn_sub_g, even_gk_w_f, even_gk_b_f, even_gk_w_b, even_gk_b_b, even_gla_norm_g, even_w_out, odd_w_in, odd_gate_b, odd_norm_g, odd_w_out, ffn_w_gu, ffn_w_down, final_g):
    n_even = even_w_in.shape[0]
    n_odd = odd_w_in.shape[0]
    r = B_GATE_RANK
    kw = B_HEADS * B_K_DIM
    gk_w = jnp.zeros((n_even, 2, GATE_PAD, kw), F32)
    gk_w = gk_w.at[:, 0, 0:r].set(even_gk_w_f).at[:, 1, r:2 * r].set(even_gk_w_b)
    p = {
        "norm1_g": norm1_g[:, None, :],
        "norm2_g": norm2_g[:, None, :],
        "final_g": final_g[None, :],
        "even_w_main": jnp.concatenate([even_w_in[:, :, :EVEN_AV], even_w_in[:, :, EVEN_REST:EVEN_MAIN]],
                                       axis=-1).astype(BF16),
        "even_w_vt": jnp.swapaxes(even_w_in[:, :, EVEN_AV:EVEN_REST], 1, 2).astype(BF16),
        "even_w_lr": jnp.stack([_pad_cols(even_w_in[j, :, EVEN_MAIN:], GATE_PAD) for j in range(n_even)]),
        "even_lam": jnp.stack([even_lam_q1, even_lam_k1, even_lam_q2, even_lam_k2], axis=1),
        "even_sub_g": even_attn_sub_g[:, :, None],
        "even_gk_w": gk_w,
        "even_gk_b": jnp.stack([even_gk_b_f, even_gk_b_b], axis=1)[:, :, None, :],
        "even_gla_norm_g": even_gla_norm_g[:, None, :],
        "even_w_out": even_w_out.astype(BF16),
        "odd_w_main": odd_w_in[:, :, :ODD_MAIN].astype(BF16),
        "odd_w_gate": jnp.stack([_pad_cols(odd_w_in[j, :, ODD_MAIN:], GATE_PAD) for j in range(n_odd)]),
        "odd_w_gate_t": jnp.swapaxes(odd_w_in[:, :, ODD_MAIN:], 1, 2),
        "odd_gate_b": odd_gate_b,
        "odd_norm_g": odd_norm_g[:, None, :],
        "odd_w_out": odd_w_out.astype(BF16),
        "ffn_w_gu": ffn_w_gu.astype(BF16),
        "ffn_w_down": ffn_w_down.astype(BF16),
    }
    nbp = c_prompt.shape[0]
    mod = _modulation(jnp.concatenate([c_prompt, c_sample], axis=0), w_mod, b_mod)
    y_prompt = _trunk(x_prompt, mod[:, :nbp], p)
    y_sample = _trunk(x_sample, mod[:, nbp:], p)
    return (y_prompt, y_sample)
```

```python
import functools
import math

import jax
import jax.numpy as jnp
from jax import lax
from jax.experimental import pallas as pl
from jax.experimental.pallas import tpu as pltpu

F32 = jnp.float32
BF16 = jnp.bfloat16
HIGHEST = lax.Precision.HIGHEST

D_MODEL = 1024
DEPTH = 2
EPS = 1e-6
A_HEADS = 4
A_QK_DIM = 64
A_V_DIM = 128
B_HEADS = 4
B_K_DIM = 64
B_V_DIM = 128
B_GATE_RANK = 16
B_GATE_NORM = 16.0
B_CHUNK = 64
C_HEADS = 4
C_QK_DIM = 128
C_V_DIM = 256
C_CHUNK = 64
D_FF = 2816
EVEN_AV = 1024
EVEN_REST = 1536
EVEN_MAIN = 3072
EVEN_AK = 512
EVEN_BQ = 1536
EVEN_BV = 2048
EVEN_BG = 2560
ODD_MAIN = 3072
ODD_CO = 2048
GATE_PAD = 128
ONES_ROWS = 16
LOG2E = 1.4426950408889634
ATTN_SUB = 256
ATTN_AHEAD = 1
ATTN_STREAMS = 8

VMEM_LIMIT = 56 * 1024 * 1024

_NT = (((1,), (1,)), ((), ()))
_TN = (((0,), (0,)), ((), ()))


def _log_sigmoid(x):
    return jnp.minimum(x, 0.0) - jnp.log1p(jnp.exp(-jnp.abs(x)))


def _silu(x):
    return x * jax.nn.sigmoid(x)


def _split_bf16(x, terms):
    parts = []
    for _ in range(terms - 1):
        hi = x.astype(BF16)
        parts.append(hi)
        x = x - hi.astype(F32)
    parts.append(x.astype(BF16))
    return parts


SPLIT_TERMS = 2


def _dot_exact_lhs(a_bf16, x):
    return sum(jnp.dot(a_bf16, part, preferred_element_type=F32) for part in _split_bf16(x, SPLIT_TERMS))


def _dot_exact_rhs(x, a_bf16):
    return sum(jnp.dot(part, a_bf16, preferred_element_type=F32) for part in _split_bf16(x, SPLIT_TERMS))


def _params(sem):
    return pltpu.CompilerParams(dimension_semantics=sem, vmem_limit_bytes=VMEM_LIMIT)


def _mod_kernel(c_ref, w_ref, b_ref, o_ref):
    s = _silu(c_ref[...])
    o_ref[0] = jnp.dot(s, w_ref[0], precision=HIGHEST, preferred_element_type=F32) + b_ref[0]


def _modulation(c_all, w_mod, b_mod):
    nb = c_all.shape[0]
    tn = 1536
    return pl.pallas_call(
        _mod_kernel,
        grid=(DEPTH, 6 * D_MODEL // tn),
        in_specs=[
            pl.BlockSpec((nb, D_MODEL), lambda i, j: (0, 0)),
            pl.BlockSpec((1, D_MODEL, tn), lambda i, j: (i, 0, j)),
            pl.BlockSpec((1, 1, tn), lambda i, j: (i, 0, j)),
        ],
        out_specs=pl.BlockSpec((1, nb, tn), lambda i, j: (i, 0, j)),
        out_shape=jax.ShapeDtypeStruct((DEPTH, nb, 6 * D_MODEL), F32),
        compiler_params=_params(("parallel", "parallel")),
        name="modulation",
    )(c_all, w_mod, b_mod.reshape(DEPTH, 1, 6 * D_MODEL))


def _norm_mod(x, g, sh, sc):
    ms = jnp.mean(x * x, axis=-1, keepdims=True)
    y = x * lax.rsqrt(ms + EPS) * g
    return y * (1.0 + sc) + sh


def _inproj_kernel(x_ref, sh_ref, sc_ref, g_ref, w_ref, wg_ref, wt_ref, *rest, n_main, n_col, chunk, with_rows,
                   row_tile):
    if with_rows:
        z_ref, zg_ref, zt_ref = rest
    else:
        kf_ref, z_ref, zg_ref, zt_ref = rest
    for r in range(x_ref.shape[1] // row_tile):
        rs = slice(r * row_tile, (r + 1) * row_tile)
        hb = _norm_mod(x_ref[0, rs, :], g_ref[...], sh_ref[0], sc_ref[0]).astype(BF16)
        for j in range(n_main // n_col):
            zc = jnp.dot(hb, w_ref[:, j * n_col:(j + 1) * n_col], preferred_element_type=F32)
            if with_rows or j == 0:
                z_ref[0, rs, j * n_col:(j + 1) * n_col] = zc.astype(BF16)
            elif j == 1:
                hw = 2 * A_QK_DIM
                dup = jnp.concatenate([zc[:, (i // 2) * hw:(i // 2 + 1) * hw] for i in range(2 * A_HEADS)], axis=1)
                pos = (lax.broadcasted_iota(jnp.int32, (row_tile, 1), 0) & (ATTN_SUB - 1)).astype(F32)
                feat = kf_ref[1:2, :] + kf_ref[2:3, :] * pos
                z_ref[0, rs, n_col:3 * n_col] = jnp.where(kf_ref[0:1, :] > 0.0, dup, feat).astype(BF16)
            else:
                z_ref[0, rs, (j + 1) * n_col:(j + 2) * n_col] = zc.astype(BF16)
        zg_ref[0, rs, :] = jnp.dot(hb, wg_ref[...].astype(BF16), preferred_element_type=F32)
        if with_rows:
            for c in range(row_tile // chunk):
                zt_ref[0, r * (row_tile // chunk) + c] = lax.dot_general(
                    wt_ref[...].astype(BF16), hb[c * chunk:(c + 1) * chunk], _NT, preferred_element_type=F32)
        else:
            for c in range(row_tile // ATTN_SUB):
                zt_ref[0, r * (row_tile // ATTN_SUB) + c] = lax.dot_general(
                    wt_ref[...], hb[c * ATTN_SUB:(c + 1) * ATTN_SUB], _NT,
                    preferred_element_type=F32).astype(BF16)


def _inproj(x, sh, sc, g, w_main, w_gate, w_t, k_feat=None, *, with_rows, tm=1024, row_tile=512, chunk=64):
    B, T, _ = x.shape
    n_main = w_main.shape[1]
    n_col = 512
    n_z = n_main if with_rows else n_main + n_col
    kern = functools.partial(_inproj_kernel, n_main=n_main, n_col=n_col, chunk=chunk, with_rows=with_rows,
                             row_tile=row_tile)
    in_specs = [
        pl.BlockSpec((1, tm, D_MODEL), lambda b, i: (b, i, 0)),
        pl.BlockSpec((1, 1, D_MODEL), lambda b, i: (b, 0, 0)),
        pl.BlockSpec((1, 1, D_MODEL), lambda b, i: (b, 0, 0)),
        pl.BlockSpec((1, D_MODEL), lambda b, i: (0, 0)),
        pl.BlockSpec((D_MODEL, n_main), lambda b, i: (0, 0)),
        pl.BlockSpec((D_MODEL, GATE_PAD), lambda b, i: (0, 0)),
    ]
    out_specs = [
        pl.BlockSpec((1, tm, n_z), lambda b, i: (b, i, 0)),
        pl.BlockSpec((1, tm, GATE_PAD), lambda b, i: (b, i, 0)),
    ]
    out_shape = [
        jax.ShapeDtypeStruct((B, T, n_z), BF16),
        jax.ShapeDtypeStruct((B, T, GATE_PAD), F32),
    ]
    args = [x, sh, sc, g, w_main, w_gate, w_t]
    nt = w_t.shape[0]
    in_specs.append(pl.BlockSpec((nt, D_MODEL), lambda b, i: (0, 0)))
    if with_rows:
        out_specs.append(pl.BlockSpec((1, tm // chunk, nt, chunk), lambda b, i: (b, i, 0, 0)))
        out_shape.append(jax.ShapeDtypeStruct((B, T // chunk, nt, chunk), F32))
    else:
        args.append(k_feat)
        in_specs.append(pl.BlockSpec(k_feat.shape, lambda b, i: (0, 0)))
        out_specs.append(pl.BlockSpec((1, tm // ATTN_SUB, nt, ATTN_SUB), lambda b, i: (b, i, 0, 0)))
        out_shape.append(jax.ShapeDtypeStruct((B, T // ATTN_SUB, nt, ATTN_SUB), BF16))
    return pl.pallas_call(
        kern,
        grid=(B, T // tm),
        in_specs=in_specs,
        out_specs=out_specs,
        out_shape=out_shape,
        compiler_params=_params(("parallel", "parallel")),
        name="inproj_rows" if with_rows else "inproj",
    )(*args)


def _attn_kernel(slope_ref, q_ref, ka_ref, kb_ref, vt_ref, lamp_ref, subg_ref, o_ref, q_scr, *, n_sub, lam_init):
    h = pl.program_id(1)
    tq = sk = ATTN_SUB
    dv = A_V_DIM
    hw = 2 * A_QK_DIM
    slope2 = slope_ref[h, 0] + slope_ref[h, 1] + slope_ref[h, 2]
    streams = range(ATTN_STREAMS)
    qis = [pl.program_id(2) * ATTN_STREAMS + st for st in streams]

    lane = lax.broadcasted_iota(jnp.int32, (1, hw), 1)
    c_off = lax.broadcasted_iota(jnp.int32, (tq, hw), 0).astype(F32)
    for c in range(2):
        own = (lane < A_QK_DIM) if c == 0 else (lane >= A_QK_DIM)
        base = (1 - c) * A_QK_DIM
        q_feat = jnp.zeros((tq, hw), F32)
        for x in range(3):
            q_feat = jnp.where(lane == base + x, slope_ref[h, x], q_feat)
            q_feat = jnp.where(lane == base + 3 + x, -c_off, q_feat)
        for st in streams:
            qs = (q_ref[0, st * tq:(st + 1) * tq, :].astype(F32) * (A_QK_DIM ** -0.5 * LOG2E)).astype(BF16)
            for v, sign in enumerate((1.0, -1.0, 0.0)):
                q_scr[st, v, c] = jnp.where(own, qs, (sign * q_feat).astype(BF16))
    rel = (lax.broadcasted_iota(jnp.int32, (sk, tq), 0)
           - lax.broadcasted_iota(jnp.int32, (sk, tq), 1)).astype(F32)
    diag_bias = -slope2 * jnp.abs(rel)
    ones = jnp.ones((ONES_ROWS, sk), BF16)
    k_refs = (ka_ref, kb_ref)

    def scores(st, j):
        qi = qis[st]
        if j == 0:
            idx, const, v = qi, jnp.zeros((), F32), 2
        else:
            idx = qi + j
            idx = jnp.where(idx >= n_sub, idx - n_sub, idx)
            before = idx < qi
            const = jnp.where(before, 1.0, -1.0) * slope2 * ((idx - qi) * sk).astype(F32)
            v = jnp.where(before, 0, 1)
        rows = pl.ds(pl.multiple_of(idx * sk, sk), sk)
        s = [lax.dot_general(k_refs[c][0, rows, :], q_scr[st, v, c], _NT, preferred_element_type=F32)
             for c in range(2)]
        if j == 0:
            s = [s_c + diag_bias for s_c in s]
        s = jnp.concatenate(s, axis=1)
        return s, jnp.max(s, axis=0, keepdims=True) + const, const, idx

    def accumulate(acc, alpha, p, idx):
        vt1 = jnp.concatenate([vt_ref[0, idx], ones], axis=0)
        return alpha * acc + jnp.dot(vt1, p, preferred_element_type=F32)

    m = [jnp.full((1, 2 * tq), -jnp.inf, F32) for _ in streams]
    acc = [jnp.zeros((dv + ONES_ROWS, 2 * tq), F32) for _ in streams]
    ahead = [[scores(st, j) for j in range(min(ATTN_AHEAD, n_sub))] for st in streams]
    pending = [None for _ in streams]
    for j in range(n_sub):
        cur = [ahead[st].pop(0) for st in streams]
        if j + ATTN_AHEAD < n_sub:
            for st in streams:
                ahead[st].append(scores(st, j + ATTN_AHEAD))
        for st in streams:
            s, m_loc, const, idx = cur[st]
            m_new = jnp.maximum(m[st], m_loc)
            alpha = jnp.exp2(m[st] - m_new)
            p = jnp.exp2(s - (m_new - const)).astype(BF16)
            m[st] = m_new
            cur[st] = (alpha, p, idx)
        for st in streams:
            if pending[st] is not None:
                acc[st] = accumulate(acc[st], *pending[st])
            pending[st] = cur[st]
    lp = lamp_ref[...]
    lam = (jnp.exp(jnp.sum(lp[0:1] * lp[1:2], axis=-1, keepdims=True))
           - jnp.exp(jnp.sum(lp[2:3] * lp[3:4], axis=-1, keepdims=True)) + lam_init)
    for st in streams:
        a = accumulate(acc[st], *pending[st])
        on = a[:dv] / a[dv:dv + 1]
        o = on[:, :tq] - lam * on[:, tq:]
        ms = jnp.mean(o * o, axis=0, keepdims=True)
        y = o * lax.rsqrt(ms + EPS) * subg_ref[...] * (1.0 - lam_init)
        o_ref[0, st * tq:(st + 1) * tq, :] = y.T.astype(o_ref.dtype)


def _alibi_slope_terms():
    slope2 = jnp.exp2(-8.0 * jnp.arange(1, A_HEADS + 1, dtype=F32) / A_HEADS) * LOG2E
    s0 = slope2.astype(BF16).astype(F32)
    s1 = (slope2 - s0).astype(BF16).astype(F32)
    s2 = (slope2 - s0 - s1).astype(BF16).astype(F32)
    return jnp.stack([s0, s1, s2], axis=1)


def _alibi_key_features():
    hw = 2 * A_QK_DIM
    terms = _alibi_slope_terms()
    lane = jnp.arange(hw)
    rows = []
    for h in range(A_HEADS):
        for c in range(2):
            base = (1 - c) * A_QK_DIM
            keep = ((lane < A_QK_DIM) if c == 0 else (lane >= A_QK_DIM)).astype(F32)
            const = jnp.zeros((hw,), F32).at[base + 3:base + 6].set(terms[h])
            pos = jnp.zeros((hw,), F32).at[base:base + 3].set(1.0)
            rows.append(jnp.stack([keep, const, pos]))
    return jnp.concatenate(rows, axis=1)


def _diff_attention(z, vt, lam_params, sub_g_col, lam_init):
    B, T, _ = z.shape
    n_sub = T // ATTN_SUB
    assert T % (ATTN_SUB * ATTN_STREAMS) == 0, T
    slopes = _alibi_slope_terms()
    kern = functools.partial(_attn_kernel, n_sub=n_sub, lam_init=lam_init)
    hw = 2 * A_QK_DIM
    return pl.pallas_call(
        kern,
        grid=(B, A_HEADS, n_sub // ATTN_STREAMS),
        in_specs=[
            pl.BlockSpec(memory_space=pltpu.SMEM),
            pl.BlockSpec((1, ATTN_STREAMS * ATTN_SUB, hw), lambda b, h, i: (b, i, h)),
            pl.BlockSpec((1, T, hw), lambda b, h, i: (b, 0, EVEN_AK // hw + 2 * h)),
            pl.BlockSpec((1, T, hw), lambda b, h, i: (b, 0, EVEN_AK // hw + 2 * h + 1)),
            pl.BlockSpec((1, n_sub, A_V_DIM, ATTN_SUB), lambda b, h, i: (b, 0, h, 0)),
            pl.BlockSpec((4, A_QK_DIM), lambda b, h, i: (0, 0)),
            pl.BlockSpec((A_V_DIM, 1), lambda b, h, i: (0, 0)),
        ],
        out_specs=pl.BlockSpec((1, ATTN_STREAMS * ATTN_SUB, A_V_DIM), lambda b, h, i: (b, i, h)),
        out_shape=jax.ShapeDtypeStruct((B, T, A_HEADS * A_V_DIM), BF16),
        scratch_shapes=[pltpu.VMEM((ATTN_STREAMS, 3, 2, ATTN_SUB, hw), BF16)],
        compiler_params=_params(("parallel", "parallel", "parallel")),
        name="diff_attention",
    )(slopes, z, z, z, vt, lam_params, sub_g_col)


def _gla_kernel(qf_ref, kf_ref, vf_ref, lrf_ref, qb_ref, kb_ref, vb_ref, lrb_ref, gw_ref, gb_ref,
                of_ref, ob_ref, s_scr, *, tg):
    L = B_CHUNK
    nchunk = tg // L
    pw = 2 * B_K_DIM
    vw = 2 * B_V_DIM

    @pl.when(pl.program_id(1) == 0)
    def _():
        s_scr[...] = jnp.zeros(s_scr.shape, F32)

    tt = lax.broadcasted_iota(jnp.int32, (L, L), 0)
    ss = lax.broadcasted_iota(jnp.int32, (L, L), 1)
    lane = lax.broadcasted_iota(jnp.int32, (1, pw), 1)
    srow = lax.broadcasted_iota(jnp.int32, (vw, pw), 0)
    slane = lax.broadcasted_iota(jnp.int32, (vw, pw), 1)
    same_head = (srow >= B_V_DIM) == (slane >= B_K_DIM)
    zero_b = jnp.zeros((), BF16)

    dirs = ((qf_ref, kf_ref, vf_ref, lrf_ref, of_ref), (qb_ref, kb_ref, vb_ref, lrb_ref, ob_ref))
    keeps = (ss <= tt, ss >= tt)
    tris = [kp.astype(BF16) for kp in keeps]
    lgs = []
    for d in range(2):
        pre = jnp.dot(dirs[d][3][0].astype(BF16), gw_ref[d].astype(BF16), preferred_element_type=F32) + gb_ref[d]
        lgs.append(_log_sigmoid(pre) * (1.0 / B_GATE_NORM))
    np_ = B_HEADS // 2
    states = [[s_scr[d, p] for p in range(np_)] for d in range(2)]
    sels = (lane < B_K_DIM, lane >= B_K_DIM)
    csl = [slice(c * L, (c + 1) * L) for c in range(nchunk)]
    ksl = [slice(p * pw, (p + 1) * pw) for p in range(np_)]
    items = [(c, d) for c in range(nchunk) for d in range(2)]
    chains = [(c, d, p) for c, d in items for p in range(np_)]
    bs = {(c, d): _dot_exact_lhs(tris[d], lgs[d][csl[c]]) for c, d in items}
    b_ends = {(c, d): bs[c, d][L - 1:L] if d == 0 else bs[c, d][0:1] for c, d in items}
    qcs = {(c, d): dirs[d][0][0, csl[c], :].astype(F32) for c, d in items}
    kcs = {(c, d): dirs[d][1][0, csl[c], :].astype(F32) for c, d in items}
    qes = {i: (qcs[i] * (B_K_DIM ** -0.5) * jnp.exp(bs[i])).astype(BF16) for i in items}
    kes = {i: (kcs[i] * jnp.exp(-bs[i])).astype(BF16) for i in items}
    kds = {i: (kcs[i] * jnp.exp(b_ends[i] - bs[i])).astype(BF16) for i in items}
    decs = {i: jnp.exp(b_ends[i]) for i in items}
    v_ps = {(c, d, p): dirs[d][2][0, csl[c], p * vw:(p + 1) * vw] for c, d, p in chains}
    atts = {(c, d, p, hh): lax.dot_general(jnp.where(sels[hh], qes[c, d][:, ksl[p]], zero_b), kes[c, d][:, ksl[p]],
                                           _NT, preferred_element_type=F32)
            for c, d, p in chains for hh in range(2)}
    atts = {key: jnp.where(keeps[key[1]], a, 0.0).astype(BF16) for key, a in atts.items()}
    intras = {(c, d, p): jnp.concatenate(
        [jnp.dot(atts[c, d, p, hh], v_ps[c, d, p][:, hh * B_V_DIM:(hh + 1) * B_V_DIM],
                 preferred_element_type=F32) for hh in range(2)], axis=-1) for c, d, p in chains}
    uts = {(c, d, p): jnp.where(same_head, lax.dot_general(v_ps[c, d, p], kds[c, d][:, ksl[p]], _TN,
                                                           preferred_element_type=F32), 0.0)
           for c, d, p in chains}
    for ci in range(nchunk):
        for d in range(2):
            c = ci if d == 0 else nchunk - 1 - ci
            for p in range(np_):
                st = states[d][p]
                inter = lax.dot_general(qes[c, d][:, ksl[p]], st.astype(BF16), _NT, preferred_element_type=F32)
                dirs[d][4][0, csl[c], p * vw:(p + 1) * vw] = (inter + intras[c, d, p]).astype(BF16)
                states[d][p] = decs[c, d][:, ksl[p]] * st + uts[c, d, p]
    for d in range(2):
        for p in range(np_):
            s_scr[d, p] = states[d][p]


def _gla(z, zlr, gw, gb, *, tg=1024):
    B, T, _ = z.shape
    nb = T // tg
    kw = B_HEADS * B_K_DIM
    vw = B_HEADS * B_V_DIM
    q_blk = EVEN_BQ // kw
    v_blk = EVEN_BV // vw
    fwd = lambda b, i: (b, i)
    bwd = lambda b, i: (b, nb - 1 - i)

    def specs(order):
        return [
            pl.BlockSpec((1, tg, kw), lambda b, i: (*order(b, i), q_blk)),
            pl.BlockSpec((1, tg, kw), lambda b, i: (*order(b, i), q_blk + 1)),
            pl.BlockSpec((1, tg, vw), lambda b, i: (*order(b, i), v_blk)),
            pl.BlockSpec((1, tg, GATE_PAD), lambda b, i: (*order(b, i), 0)),
        ]

    kern = functools.partial(_gla_kernel, tg=tg)
    return pl.pallas_call(
        kern,
        grid=(B, nb),
        in_specs=specs(fwd) + specs(bwd) + [
            pl.BlockSpec((2, GATE_PAD, kw), lambda b, i: (0, 0, 0)),
            pl.BlockSpec((2, 1, kw), lambda b, i: (0, 0, 0)),
        ],
        out_specs=[
            pl.BlockSpec((1, tg, vw), lambda b, i: (b, i, 0)),
            pl.BlockSpec((1, tg, vw), lambda b, i: (b, nb - 1 - i, 0)),
        ],
        out_shape=[jax.ShapeDtypeStruct((B, T, vw), BF16)] * 2,
        scratch_shapes=[pltpu.VMEM((2, B_HEADS // 2, 2 * B_V_DIM, 2 * B_K_DIM), F32)],
        compiler_params=_params(("parallel", "arbitrary")),
        name="gla",
    )(z, z, z, zlr, z, z, z, zlr, gw, gb)


def _head_rmsnorm(o, g, width):
    parts = []
    for h in range(o.shape[-1] // width):
        oh = o[:, h * width:(h + 1) * width]
        ms = jnp.mean(oh * oh, axis=-1, keepdims=True)
        parts.append(oh * lax.rsqrt(ms + EPS) * g[:, h * width:(h + 1) * width])
    return jnp.concatenate(parts, axis=-1)


def _mix_ffn_kernel(*refs, even, final_norm, ff_chunk, row_tile):
    if even:
        a_ref, of_ref, ob_ref, gate_ref = refs[:4]
        refs = refs[4:]
    else:
        hf_ref, hb_ref, gate_ref = refs[:3]
        refs = refs[3:]
    gn_ref, x_ref, g1_ref, wo_ref, sh_ref, sc_ref, g_ref, g2_ref, wgu_ref, wd_ref, fg_ref, o_ref = refs
    tm = x_ref.shape[1]
    for r in range(tm // row_tile):
        rs = slice(r * row_tile, (r + 1) * row_tile)
        if even:
            o = _head_rmsnorm(of_ref[0, rs, :].astype(F32) + ob_ref[0, rs, :].astype(F32), gn_ref[...], B_V_DIM)
            b_out = (o * _silu(gate_ref[0, rs, :].astype(F32))).astype(BF16)
            mix = jnp.concatenate([a_ref[0, rs, :], b_out], axis=-1)
        else:
            ht = _head_rmsnorm(hf_ref[0, rs, :].astype(F32) + hb_ref[0, rs, :].astype(F32), gn_ref[...], C_V_DIM)
            mix = (jax.nn.sigmoid(gate_ref[0, rs, :].astype(F32)) * ht).astype(BF16)
        x1 = x_ref[0, rs, :] + g1_ref[0] * jnp.dot(mix, wo_ref[...], preferred_element_type=F32)
        hb = _norm_mod(x1, g_ref[...], sh_ref[0], sc_ref[0]).astype(BF16)
        acts = []
        for c in range(D_FF // ff_chunk):
            gate = jnp.dot(hb, wgu_ref[:, c * ff_chunk:(c + 1) * ff_chunk], preferred_element_type=F32)
            up = jnp.dot(hb, wgu_ref[:, D_FF + c * ff_chunk:D_FF + (c + 1) * ff_chunk],
                         preferred_element_type=F32)
            acts.append((_silu(gate) * up).astype(BF16))
        act = jnp.concatenate(acts, axis=-1)
        y = x1 + g2_ref[0] * jnp.dot(act, wd_ref[...], preferred_element_type=F32)
        if final_norm:
            ms = jnp.mean(y * y, axis=-1, keepdims=True)
            y = y * lax.rsqrt(ms + EPS) * fg_ref[...]
        o_ref[0, rs, :] = y


def _mix_ffn(mixer_outs, z, gate_blk, gn, x, g1, w_out, sh, sc, g, g2, w_gu, w_down, final_g,
             *, even, final_norm, tm=1024, ff_chunk=256, row_tile=512):
    B, T, _ = x.shape
    tok = lambda b, i: (b, i, 0)
    per_b = lambda b, i: (b, 0, 0)
    const = lambda b, i: (0, 0)
    resident = functools.partial(pl.BlockSpec, index_map=const, pipeline_mode=pl.Buffered(1))
    nm = mixer_outs[0].shape[-1]
    in_specs = [pl.BlockSpec((1, tm, nm), tok) for _ in mixer_outs]
    in_specs += [
        pl.BlockSpec((1, tm, nm), lambda b, i: (b, i, gate_blk)),
        pl.BlockSpec((1, nm), const),
        pl.BlockSpec((1, tm, D_MODEL), tok),
        pl.BlockSpec((1, 1, D_MODEL), per_b),
        resident(w_out.shape),
        pl.BlockSpec((1, 1, D_MODEL), per_b),
        pl.BlockSpec((1, 1, D_MODEL), per_b),
        pl.BlockSpec((1, D_MODEL), const),
        pl.BlockSpec((1, 1, D_MODEL), per_b),
        resident(w_gu.shape),
        resident(w_down.shape),
        pl.BlockSpec((1, D_MODEL), const),
    ]
    kern = functools.partial(_mix_ffn_kernel, even=even, final_norm=final_norm, ff_chunk=ff_chunk,
                             row_tile=row_tile)
    return pl.pallas_call(
        kern,
        grid=(B, T // tm),
        in_specs=in_specs,
        out_specs=pl.BlockSpec((1, tm, D_MODEL), tok),
        out_shape=jax.ShapeDtypeStruct((B, T, D_MODEL), F32),
        compiler_params=_params(("parallel", "parallel")),
        name=("mix_ffn_even" if even else "mix_ffn_odd") + ("_final" if final_norm else ""),
    )(*mixer_outs, z, gn, x, g1, w_out, sh, sc, g, g2, w_gu, w_down, final_g)


def _mlstm_kernel(gbr_ref, gbc_ref, qf_ref, kf_ref, vf_ref, gcf_ref, grf_ref, qb_ref, kb_ref, vb_ref, gcb_ref,
                  grb_ref, of_ref, ob_ref, c_scr, m_scr, *, tg):
    L = C_CHUNK
    nchunk = tg // L
    dk, dv = C_QK_DIM, C_V_DIM

    @pl.when(pl.program_id(1) == 0)
    def _():
        c_scr[...] = jnp.zeros(c_scr.shape, F32)
        m_scr[...] = jnp.zeros(m_scr.shape, F32)

    tt = lax.broadcasted_iota(jnp.int32, (L, L), 0)
    ss = lax.broadcasted_iota(jnp.int32, (L, L), 1)

    dirs = ((qf_ref, kf_ref, vf_ref, gcf_ref, grf_ref, of_ref),
            (qb_ref, kb_ref, vb_ref, gcb_ref, grb_ref, ob_ref))
    keeps = (ss <= tt, ss >= tt)
    tri_col = [keeps[0].astype(BF16), keeps[1].astype(BF16)]
    tri_row = [keeps[1].astype(BF16), keeps[0].astype(BF16)]
    glane = lax.broadcasted_iota(jnp.int32, (1, GATE_PAD), 1)
    f_lane = jnp.logical_and((glane & C_HEADS) != 0, glane < 4 * C_HEADS)
    f_row = (lax.broadcasted_iota(jnp.int32, (4 * C_HEADS, 1), 0) & C_HEADS) != 0
    csl = [slice(c * L, (c + 1) * L) for c in range(nchunk)]
    items = [(c, d) for c in range(nchunk) for d in range(2)]
    chains = [(c, d, h) for c, d in items for h in range(C_HEADS)]
    fi = {(d, h): (2 * d + 1) * C_HEADS + h for d in range(2) for h in range(C_HEADS)}

    Gc = []
    for d in range(2):
        g = dirs[d][3][0] + gbr_ref[...]
        Gc.append(jnp.where(f_lane, _log_sigmoid(g), g))
    Gr = {}
    for c, d in items:
        g = dirs[d][4][0, c] + gbc_ref[...]
        Gr[c, d] = jnp.where(f_row, _log_sigmoid(g), g)
    Gi = [pltpu.roll(Gc[d], C_HEADS, axis=1) for d in range(2)]
    Gri = {i: jnp.roll(Gr[i], C_HEADS, axis=0) for i in items}
    Fc = {(c, d): _dot_exact_lhs(tri_col[d], Gc[d][csl[c]]) for c, d in items}
    Fr = {(c, d): _dot_exact_rhs(Gr[c, d], tri_row[d]) for c, d in items}
    FLl = {(c, d): Fc[c, d][L - 1:L] if d == 0 else Fc[c, d][0:1] for c, d in items}
    WL = {(c, d): FLl[c, d] - Fc[c, d] + Gi[d][csl[c]] for c, d in items}
    wl_max = {i: jnp.max(WL[i], axis=0, keepdims=True) for i in items}
    w_r = {i: Gri[i] - Fr[i] for i in items}
    neg = jnp.full((L, GATE_PAD), -jnp.inf, F32)
    P = {}
    for c, d in items:
        x = Gi[d][csl[c]] - Fc[c, d]
        sh = 1
        while sh < L:
            shifted = (jnp.concatenate([neg[:sh], x[:L - sh]], axis=0) if d == 0
                       else jnp.concatenate([x[sh:], neg[:sh]], axis=0))
            x = jnp.maximum(x, shifted)
            sh *= 2
        P[c, d] = x
    m_old_l, m_new_l = {}, {}
    ml = [m_scr[d] for d in range(2)]
    for ci in range(nchunk):
        for d in range(2):
            c = ci if d == 0 else nchunk - 1 - ci
            m_old_l[c, d] = ml[d]
            ml[d] = jnp.maximum(FLl[c, d] + ml[d], wl_max[c, d])
            m_new_l[c, d] = ml[d]
    MT = {i: Fc[i] + jnp.maximum(m_old_l[i], P[i]) for i in items}
    U = {i: Fc[i] - MT[i] for i in items}
    INTER = {i: jnp.exp(Fc[i] + m_old_l[i] - MT[i]) for i in items}
    EMT = {i: jnp.exp(-MT[i]) for i in items}
    WS = {i: jnp.exp(WL[i] - m_new_l[i]) for i in items}
    DEC = {i: jnp.exp(FLl[i] + m_old_l[i] - m_new_l[i]) for i in items}
    na = dv + dk
    col = lambda X, k_, n: jnp.broadcast_to(X[k_[0], k_[1]][:, fi[k_[1], k_[2]]:fi[k_[1], k_[2]] + 1], (L, n))
    qc = {(c, d, h): dirs[d][0][0, csl[c], h * dk:(h + 1) * dk] for c, d, h in chains}
    kf = {(c, d, h): dirs[d][1][0, csl[c], h * dk:(h + 1) * dk].astype(F32) * (dk ** -0.5) for c, d, h in chains}
    ones = jnp.ones((L, dk), BF16)
    va = {(c, d, h): jnp.concatenate([dirs[d][2][0, csl[c], h * dv:(h + 1) * dv], ones], axis=1)
          for c, d, h in chains}
    qk = {k_: lax.dot_general(qc[k_], kf[k_].astype(BF16), _NT, preferred_element_type=F32) for k_ in chains}
    W = {k_: (jnp.exp(jnp.where(keeps[k_[1]], col(U, k_, L) + w_r[k_[0], k_[1]][fi[k_[1], k_[2]]:fi[k_[1], k_[2]] + 1],
                                -jnp.inf)) * qk[k_]).astype(BF16) for k_ in chains}
    Wv = {k_: jnp.dot(W[k_], va[k_], preferred_element_type=F32) for k_ in chains}
    kw = {k_: (kf[k_] * col(WS, k_, dk)).astype(BF16) for k_ in chains}
    kv = {k_: lax.dot_general(kw[k_], va[k_], _TN, preferred_element_type=F32) for k_ in chains}
    inter = {k_: col(INTER, k_, dk) for k_ in chains}
    e_mt = {k_: col(EMT, k_, dk) for k_ in chains}
    dec = {(c, d, h): DEC[c, d][:, fi[d, h]:fi[d, h] + 1] for c, d, h in chains}
    Cs = {(d, h): c_scr[d, h] for d in range(2) for h in range(C_HEADS)}
    for ci in range(nchunk):
        for d in range(2):
            c = ci if d == 0 else nchunk - 1 - ci
            for h in range(C_HEADS):
                k_ = (c, d, h)
                C = Cs[d, h]
                g = inter[k_]
                num = (jnp.concatenate([g] * (na // dk), axis=1)
                       * jnp.dot(qc[k_], C.astype(BF16), preferred_element_type=F32) + Wv[k_])
                den = jnp.maximum(jnp.abs(num[:, dv:]), e_mt[k_])
                dirs[d][5][0, csl[c], h * dv:(h + 1) * dv] = (
                    num[:, :dv] / jnp.concatenate([den] * (dv // dk), axis=1)).astype(BF16)
                Cs[d, h] = dec[k_] * C + kv[k_]
    for d in range(2):
        for h in range(C_HEADS):
            c_scr[d, h] = Cs[d, h]
        m_scr[d] = ml[d]


def _mlstm(z, gcol, grow, gate_b, *, tg=512):
    B, T, _ = z.shape
    nb = T // tg
    qw = C_HEADS * C_QK_DIM
    vw = C_HEADS * C_V_DIM
    ng = grow.shape[2]
    fwd = lambda b, i: (b, i)
    bwd = lambda b, i: (b, nb - 1 - i)

    def specs(order):
        return [
            pl.BlockSpec((1, tg, qw), lambda b, i: (*order(b, i), 0)),
            pl.BlockSpec((1, tg, qw), lambda b, i: (*order(b, i), 1)),
            pl.BlockSpec((1, tg, vw), lambda b, i: (*order(b, i), 1)),
            pl.BlockSpec((1, tg, GATE_PAD), lambda b, i: (*order(b, i), 0)),
            pl.BlockSpec((1, tg // C_CHUNK, ng, C_CHUNK), lambda b, i: (*order(b, i), 0, 0)),
        ]

    kern = functools.partial(_mlstm_kernel, tg=tg)
    return pl.pallas_call(
        kern,
        grid=(B, nb),
        in_specs=[pl.BlockSpec((1, GATE_PAD), lambda b, i: (0, 0)),
                  pl.BlockSpec((ng, 1), lambda b, i: (0, 0))] + specs(fwd) + specs(bwd),
        out_specs=[
            pl.BlockSpec((1, tg, vw), lambda b, i: (b, i, 0)),
            pl.BlockSpec((1, tg, vw), lambda b, i: (b, nb - 1 - i, 0)),
        ],
        out_shape=[jax.ShapeDtypeStruct((B, T, vw), BF16)] * 2,
        scratch_shapes=[
            pltpu.VMEM((2, C_HEADS, C_QK_DIM, C_V_DIM + C_QK_DIM), F32),
            pltpu.VMEM((2, 1, GATE_PAD), F32),
        ],
        compiler_params=_params(("parallel", "arbitrary")),
        name="mlstm",
    )(_pad_cols(gate_b[None, :], GATE_PAD), gate_b[:, None], z, z, z, gcol, grow, z, z, z, gcol, grow)


def _pad_cols(w, n):
    return jnp.pad(w, ((0, 0), (0, n - w.shape[1])))


def _trunk(x, mod, p):
    for i in range(DEPTH):
        sh1, sc1, g1, sh2, sc2, g2 = [m[:, None, :] for m in jnp.split(mod[i], 6, axis=-1)]
        if i % 2 == 0:
            j = i // 2
            lam_init = 0.8 - 0.6 * math.exp(-0.3 * i)
            z, zlr, vt = _inproj(x, sh1, sc1, p["norm1_g"][i], p["even_w_main"][j], p["even_w_lr"][j],
                                 p["even_w_vt"][j], _alibi_key_features(), with_rows=False)
            a = _diff_attention(z, vt, p["even_lam"][j], p["even_sub_g"][j], lam_init)
            of, ob = _gla(z, zlr, p["even_gk_w"][j], p["even_gk_b"][j])
            mixer, gate_blk, gn, w_out = (a, of, ob), EVEN_BG // a.shape[-1], p["even_gla_norm_g"][j], p["even_w_out"][j]
        else:
            j = i // 2
            z, gcol, grow = _inproj(x, sh1, sc1, p["norm1_g"][i], p["odd_w_main"][j], p["odd_w_gate"][j],
                                    p["odd_w_gate_t"][j], with_rows=True, chunk=C_CHUNK)
            hf, hb = _mlstm(z, gcol, grow, p["odd_gate_b"][j])
            mixer, gate_blk, gn, w_out = (hf, hb), ODD_CO // hf.shape[-1], p["odd_norm_g"][j], p["odd_w_out"][j]
        x = _mix_ffn(mixer, z, gate_blk, gn, x, g1, w_out, sh2, sc2, p["norm2_g"][i], g2, p["ffn_w_gu"][i],
                     p["ffn_w_down"][i], p["final_g"], even=(i % 2 == 0), final_norm=(i == DEPTH - 1))
    return x


def kernel(x_prompt, x_sample, c_prompt, c_sample, w_mod, b_mod, norm1_g, norm2_g, even_w_in, even_lam_q1, even_lam_k1, even_lam_q2, even_lam_k2, even_attn_sub_g, even_gk_w_f, even_gk_b_f, even_gk_w_b, even_gk_b_b, even_gla_norm_g, even_w_out, odd_w_in, odd_gate_b, odd_norm_g, odd_w_out, ffn_w_gu, ffn_w_down, final_g):
    n_even = even_w_in.shape[0]
    n_odd = odd_w_in.shape[0]
    r = B_GATE_RANK
    kw = B_HEADS * B_K_DIM
    gk_w = jnp.zeros((n_even, 2, GATE_PAD, kw), F32)
    gk_w = gk_w.at[:, 0, 0:r].set(even_gk_w_f).at[:, 1, r:2 * r].set(even_gk_w_b)
    p = {
        "norm1_g": norm1_g[:, None, :],
        "norm2_g": norm2_g[:, None, :],
        "final_g": final_g[None, :],
        "even_w_main": jnp.concatenate([even_w_in[:, :, :EVEN_AV], even_w_in[:, :, EVEN_REST:EVEN_MAIN]],
                                       axis=-1).astype(BF16),
        "even_w_vt": jnp.swapaxes(even_w_in[:, :, EVEN_AV:EVEN_REST], 1, 2).astype(BF16),
        "even_w_lr": jnp.stack([_pad_cols(even_w_in[j, :, EVEN_MAIN:], GATE_PAD) for j in range(n_even)]),
        "even_lam": jnp.stack([even_lam_q1, even_lam_k1, even_lam_q2, even_lam_k2], axis=1),
        "even_sub_g": even_attn_sub_g[:, :, None],
        "even_gk_w": gk_w,
        "even_gk_b": jnp.stack([even_gk_b_f, even_gk_b_b], axis=1)[:, :, None, :],
        "even_gla_norm_g": even_gla_norm_g[:, None, :],
        "even_w_out": even_w_out.astype(BF16),
        "odd_w_main": odd_w_in[:, :, :ODD_MAIN].astype(BF16),
        "odd_w_gate": jnp.stack([_pad_cols(odd_w_in[j, :, ODD_MAIN:], GATE_PAD) for j in range(n_odd)]),
        "odd_w_gate_t": jnp.swapaxes(odd_w_in[:, :, ODD_MAIN:], 1, 2),
        "odd_gate_b": odd_gate_b,
        "odd_norm_g": odd_norm_g[:, None, :],
        "odd_w_out": odd_w_out.astype(BF16),
        "ffn_w_gu": ffn_w_gu.astype(BF16),
        "ffn_w_down": ffn_w_down.astype(BF16),
    }
    nbp = c_prompt.shape[0]
    mod = _modulation(jnp.concatenate([c_prompt, c_sample], axis=0), w_mod, b_mod)
    y_prompt = _trunk(x_prompt, mod[:, :nbp], p)
    y_sample = _trunk(x_sample, mod[:, nbp:], p)
    return (y_prompt, y_sample)
```
